```python
import jax
import jax.numpy as jnp
from jax import lax
import numpy as np

D_MODEL = 1024
BATCH = 16
SEQ = 256
DEPTH = 2
DEC_BATCH = 4
DEC_SEQ = 2048
PAST_LEN = 512

GRID_W = 64
MIX_W = D_MODEL // 2
N_BRANCH = 3
EPS = 1e-6
HEAD_DIM = 128
ATT_HEADS = MIX_W // HEAD_DIM
ATT_KV_HEADS = ATT_HEADS // 2
Q_BLOCK = 128
ROPE_THETA = 10000.0
RET_DK = HEAD_DIM
RET_DV = HEAD_DIM
RET_HEADS = MIX_W // RET_DV
RET_CHUNK = 128
RWKV_N = 64
RWKV_HEADS = MIX_W // RWKV_N
RWKV_W_RANK = 64
RWKV_A_RANK = 64
RWKV_G_RANK = 128
RWKV_LN_EPS = 64e-5
N_EXPERTS = 32
TOP_K = 4
D_FF = D_MODEL
SWIGLU_LIMIT = 7.0
SWIGLU_ALPHA = 1.702
RET_COLS = 4 * MIX_W
RWKV_COLS = 3 * MIX_W + RWKV_W_RANK + RWKV_A_RANK + RWKV_G_RANK
ATT_COLS = (ATT_HEADS + 2 * ATT_KV_HEADS) * HEAD_DIM
GATE_COLS = N_BRANCH * D_MODEL
P_IN = RET_COLS + RWKV_COLS + ATT_COLS + GATE_COLS

kernel_name = 'hybrid_flow_prefix_trunk'


def rev(t):
    return t[:, ::-1]


def rms_norm(x, w):
    xf = x.astype(jnp.float32)
    y = xf * lax.rsqrt(jnp.mean(xf * xf, axis=-1, keepdims=True) + EPS)
    return y.astype(x.dtype) * w


def head_norm(y, eps):
    mu = jnp.mean(y, axis=-1, keepdims=True)
    var = jnp.mean(jnp.square(y - mu), axis=-1, keepdims=True)
    return (y - mu) * lax.rsqrt(var + eps)


def grid_angles(n_tok, dim):
    rows = n_tok // GRID_W
    row = jnp.repeat(jnp.arange(rows, dtype=jnp.float32), GRID_W)
    col = jnp.tile(jnp.arange(GRID_W, dtype=jnp.float32), rows)
    n_freq = dim // 4
    inv = ROPE_THETA ** (-jnp.arange(n_freq, dtype=jnp.float32) / n_freq)
    return jnp.concatenate([row[:, None] * inv, col[:, None] * inv], axis=-1)


def apply_rope(x, ang):
    xf = x.astype(jnp.float32).reshape(x.shape[:-1] + (-1, 2))
    x0, x1 = xf[..., 0], xf[..., 1]
    cos = jnp.cos(ang)[None, :, None, :]
    sin = jnp.sin(ang)[None, :, None, :]
    out = jnp.stack([x0 * cos - x1 * sin, x0 * sin + x1 * cos], axis=-1)
    return out.reshape(x.shape).astype(x.dtype)


def centred_shift(u, mu):
    zero = jnp.zeros_like(u[:, :1])
    prev = jnp.concatenate([zero, u[:, :-1]], axis=1)
    nxt = jnp.concatenate([u[:, 1:], zero], axis=1)
    return u + mu[0] * (prev - u) + mu[1] * (nxt - u)


def retention_scan(q, k, v, log_g, s0):
    B, T, H, Dk = q.shape
    Dv = v.shape[-1]
    C = RET_CHUNK
    n = T // C

    def chunks(t):
        return t.reshape(B, n, C, H, t.shape[-1]).transpose(1, 0, 3, 2, 4)

    idx = jnp.arange(C, dtype=jnp.float32)
    diff = idx[:, None] - idx[None, :]
    decay_in = jnp.where(diff >= 0, jnp.exp(log_g[:, None, None] * jnp.maximum(diff, 0.0)), 0.0)
    xi = jnp.exp(log_g[:, None] * (idx + 1.0))[:, :, None]
    zeta = jnp.exp(log_g[:, None] * (C - 1.0 - idx))[:, :, None]
    g_chunk = jnp.exp(log_g * C)[:, None, None]

    def step(S, inp):
        qi, ki, vi = inp
        att = jnp.einsum('bhid,bhjd->bhij', qi, ki) * decay_in
        o = jnp.einsum('bhij,bhjv->bhiv', att, vi) + jnp.einsum('bhid,bhdv->bhiv', qi, S) * xi
        S = S * g_chunk + jnp.einsum('bhjd,bhjv->bhdv', ki * zeta, vi)
        return S, o

    S, o = lax.scan(step, s0, (chunks(q), chunks(k), chunks(v)))
    return o.transpose(1, 0, 3, 2, 4).reshape(B, T, H, Dv), S


def rwkv_scan(r, w, k, v, a, b, s0):
    def step(S, inp):
        rt, wt, kt, vt, at, bt = inp
        sa = jnp.einsum('bhvk,bhk->bhv', S, at)
        S = S * wt[:, :, None, :] + sa[..., None] * bt[:, :, None, :] + vt[..., None] * kt[:, :, None, :]
        return S, jnp.einsum('bhvk,bhk->bhv', S, rt)

    xs = tuple(jnp.swapaxes(t, 0, 1) for t in (r, w, k, v, a, b))
    S, y = lax.scan(step, s0, xs)
    return jnp.swapaxes(y, 0, 1), S


def attend(q, k, v):
    B, Tq, Hq, Dh = q.shape
    Hkv = k.shape[2]
    G = Hq // Hkv
    nb = Tq // Q_BLOCK
    scale = Dh ** -0.5
    qb = q.reshape(B, nb, Q_BLOCK, Hkv, G, Dh).transpose(1, 0, 2, 3, 4, 5)

    def block(qi):
        s = jnp.einsum('bqhgd,bkhd->bhgqk', qi, k).astype(jnp.float32) * scale
        pr = jax.nn.softmax(s, axis=-1).astype(v.dtype)
        return jnp.einsum('bhgqk,bkhd->bqhgd', pr, v)

    o = lax.map(block, qb)
    return o.transpose(1, 0, 2, 3, 4, 5).reshape(B, Tq, Hq * Dh)


def mixer(h, p, latent, ctx):
    B, T, _ = h.shape
    dt = h.dtype
    f32 = jnp.float32
    z = h @ p['w_in']
    zr, zw, za, zg = jnp.split(z, [RET_COLS, RET_COLS + RWKV_COLS, RET_COLS + RWKV_COLS + ATT_COLS], axis=-1)
    ang = grid_angles(T, HEAD_DIM) if latent else None

    rq, rk, rv, rg = jnp.split(zr, 4, axis=-1)
    rq = rq.reshape(B, T, RET_HEADS, RET_DK)
    rk = rk.reshape(B, T, RET_HEADS, RET_DK)
    if latent:
        rq, rk = apply_rope(rq, ang), apply_rope(rk, ang)
    rq = rq.astype(f32)
    rk = rk.astype(f32) * (RET_DK ** -0.5)
    rv = rv.astype(f32).reshape(B, T, RET_HEADS, RET_DV)
    log_g = -p['ret_decay'].astype(f32)
    s_ret0 = ctx['state_ret'].astype(f32) if latent else jnp.zeros((B, 2, RET_HEADS, RET_DK, RET_DV), f32)
    yf, sf = retention_scan(rq, rk, rv, log_g[0], s_ret0[:, 0])
    yb, sb = retention_scan(rev(rq), rev(rk), rev(rv), log_g[1], s_ret0[:, 1])
    y_ret = head_norm(yf + rev(yb), EPS).reshape(B, T, MIX_W)
    o_ret = (jax.nn.silu(rg.astype(f32)) * y_ret).astype(dt)

    zw = centred_shift(zw, p['rwkv_mu'])
    wr, wk, wv, wdn, adn, gdn = jnp.split(
        zw, [MIX_W, 2 * MIX_W, 3 * MIX_W, 3 * MIX_W + RWKV_W_RANK, 3 * MIX_W + RWKV_W_RANK + RWKV_A_RANK], axis=-1)

    def heads(t):
        return t.astype(f32).reshape(B, T, RWKV_HEADS, RWKV_N)

    icl = jax.nn.sigmoid(p['rwkv_a0'] + adn @ p['rwkv_a_up'])
    gate_out = jax.nn.sigmoid(gdn) @ p['rwkv_g_up']
    kk = heads(wk * p['rwkv_k_k'])
    kk = kk * lax.rsqrt(jnp.sum(kk * kk, axis=-1, keepdims=True) + 1e-12)
    r = heads(wr)
    k = heads(wk * (1 + (icl - 1) * p['rwkv_k_a']))
    v = heads(wv)
    a = heads(icl)

    def decay(d):
        wl = (p['rwkv_w0'][d] + jnp.tanh(wdn) @ p['rwkv_w_up'][d]).astype(f32)
        return heads(jnp.exp(-jnp.exp(-jax.nn.softplus(-wl) - 0.5)))

    s_rw0 = ctx['state_rwkv'].astype(f32) if latent else jnp.zeros((B, 2, RWKV_HEADS, RWKV_N, RWKV_N), f32)
    kb = kk * a
    vf, Sf = rwkv_scan(r, decay(0), k, v, -kk, kb, s_rw0[:, 0])
    vb, Sb = rwkv_scan(rev(r), rev(decay(1)), rev(k), rev(v), rev(-kk), rev(kb), s_rw0[:, 1])
    y = head_norm(vf + rev(vb), RWKV_LN_EPS) * p['rwkv_ln_w'].astype(f32).reshape(RWKV_HEADS, RWKV_N)
    y = y + jnp.sum(r * k * p['rwkv_r_k'].astype(f32), axis=-1, keepdims=True) * v
    o_rwkv = (y.reshape(B, T, MIX_W) * gate_out.astype(f32)).astype(dt)

    aq, ak, av = jnp.split(za, [ATT_HEADS * HEAD_DIM, (ATT_HEADS + ATT_KV_HEADS) * HEAD_DIM], axis=-1)
    aq = rms_norm(aq.reshape(B, T, ATT_HEADS, HEAD_DIM), p['q_norm_w'])
    ak = rms_norm(ak.reshape(B, T, ATT_KV_HEADS, HEAD_DIM), p['k_norm_w'])
    av = av.reshape(B, T, ATT_KV_HEADS, HEAD_DIM)
    if latent:
        aq = apply_rope(aq, ang)
        keys = jnp.concatenate([ctx['k'].astype(dt), apply_rope(ak, ang)], axis=1)
        vals = jnp.concatenate([ctx['v'].astype(dt), av], axis=1)
    else:
        keys, vals = ak, av
    o_att = attend(aq, keys, vals)

    gates = jax.nn.sigmoid(zg.astype(f32)).reshape(B, T, N_BRANCH, D_MODEL)
    branches = jnp.stack([o_ret, o_rwkv, o_att], axis=2)
    proj = jnp.einsum('btnm,nmd->btnd', branches, p['w_branch'])
    out = jnp.sum(gates * proj.astype(f32), axis=2).astype(dt) @ p['w_out']
    if latent:
        return out, None
    return out, (ak, av, jnp.stack([sf, sb], axis=1), jnp.stack([Sf, Sb], axis=1))


def moe(h, p):
    shp = h.shape
    x = h.reshape(-1, D_MODEL)
    logits = (x @ p['router_w'] + p['router_b']).astype(jnp.float32)
    top_v, top_i = lax.top_k(logits, TOP_K)
    top_p = jax.nn.softmax(top_v, axis=-1)
    gates = jnp.sum(jax.nn.one_hot(top_i, N_EXPERTS, dtype=jnp.float32) * top_p[..., None], axis=1)

    def expert(y, inp):
        wgu, bgu, wdn, bdn, g = inp
        gu = x @ wgu + bgu
        gate = jnp.minimum(gu[:, :D_FF], SWIGLU_LIMIT)
        up = jnp.clip(gu[:, D_FF:], -SWIGLU_LIMIT, SWIGLU_LIMIT)
        act = (up + 1) * gate * jax.nn.sigmoid(SWIGLU_ALPHA * gate)
        return y + g[:, None] * (act @ wdn + bdn).astype(jnp.float32), None

    y, _ = lax.scan(expert, jnp.zeros(x.shape, jnp.float32),
                    (p['w_gate_up'], p['b_gate_up'], p['w_down'], p['b_down'], gates.T))
    return y.astype(h.dtype).reshape(shp)


def layer(x, cond, p, latent, ctx):
    m = jax.nn.silu(cond) @ p['mod_w'] + p['mod_b']
    sh1, sc1, g1, sh2, sc2, g2 = [t[..., None, :] for t in jnp.split(m, 6, axis=-1)]
    h = rms_norm(x, p['norm1_w']) * (1 + sc1) + sh1
    out, ctx_out = mixer(h, p, latent, ctx)
    x = x + g1 * out
    h = rms_norm(x, p['norm2_w']) * (1 + sc2) + sh2
    x = x + g2 * moe(h, p)
    return x, ctx_out


def setup_inputs(seed: int = 0) -> dict:
    key = jax.random.key(seed)
    ks = iter(jax.random.split(key, 48))

    def nrm(shape, scale=1.0):
        return jax.random.normal(next(ks), shape, jnp.float32) * scale

    L, D, E, F = DEPTH, D_MODEL, N_EXPERTS, D_FF
    base_decay = jnp.asarray(-np.log(1.0 - 2.0 ** (-5.0 - np.arange(RET_HEADS))), jnp.float32)
    inp = {}
    inp['x_prompt'] = nrm((BATCH, SEQ, D))
    inp['x_sample'] = nrm((DEC_BATCH, DEC_SEQ, D))
    inp['cache_k'] = nrm((DEC_BATCH, DEPTH, PAST_LEN, ATT_KV_HEADS, HEAD_DIM))
    inp['cache_v'] = nrm((DEC_BATCH, DEPTH, PAST_LEN, ATT_KV_HEADS, HEAD_DIM))
    inp['state_ret'] = nrm((DEC_BATCH, DEPTH, 2, RET_HEADS, RET_DK, RET_DV))
    inp['state_rwkv'] = nrm((DEC_BATCH, DEPTH, 2, RWKV_HEADS, RWKV_N, RWKV_N))
    inp['c'] = nrm((DEC_BATCH, D))
    inp['c_ctx'] = nrm((D,))
    inp['norm1_w'] = 1.0 + nrm((L, D), 0.1)
    inp['norm2_w'] = 1.0 + nrm((L, D), 0.1)
    inp['mod_w'] = nrm((L, D, 6 * D), 0.5 * D ** -0.5)
    inp['mod_b'] = nrm((L, 6 * D), 0.02)
    inp['w_in'] = nrm((L, D, P_IN), D ** -0.5)
    inp['ret_decay'] = base_decay * jnp.exp(nrm((L, 2, RET_HEADS), 0.1))
    inp['rwkv_mu'] = jax.random.uniform(next(ks), (L, 2, RWKV_COLS), jnp.float32, 0.05, 0.5)
    inp['rwkv_w0'] = jax.random.uniform(next(ks), (L, 2, MIX_W), jnp.float32, -6.0, 1.0)
    inp['rwkv_w_up'] = nrm((L, 2, RWKV_W_RANK, MIX_W), 0.3 * RWKV_W_RANK ** -0.5)
    inp['rwkv_a0'] = nrm((L, MIX_W), 0.1)
    inp['rwkv_a_up'] = nrm((L, RWKV_A_RANK, MIX_W), 0.3 * RWKV_A_RANK ** -0.5)
    inp['rwkv_g_up'] = nrm((L, RWKV_G_RANK, MIX_W), RWKV_G_RANK ** -0.5)
    inp['rwkv_k_k'] = 0.85 + nrm((L, MIX_W), 0.05)
    inp['rwkv_k_a'] = 1.0 + nrm((L, MIX_W), 0.05)
    inp['rwkv_r_k'] = nrm((L, RWKV_HEADS, RWKV_N), 0.1)
    inp['rwkv_ln_w'] = 1.0 + nrm((L, MIX_W), 0.1)
    inp['q_norm_w'] = 1.0 + nrm((L, HEAD_DIM), 0.1)
    inp['k_norm_w'] = 1.0 + nrm((L, HEAD_DIM), 0.1)
    inp['w_branch'] = nrm((L, N_BRANCH, MIX_W, D), MIX_W ** -0.5)
    inp['w_out'] = nrm((L, D, D), D ** -0.5)
    inp['router_w'] = nrm((L, D, E), D ** -0.5)
    inp['router_b'] = nrm((L, E), 0.01)
    inp['w_gate_up'] = nrm((L, E, D, 2 * F), D ** -0.5)
    inp['b_gate_up'] = nrm((L, E, 2 * F), 0.01)
    inp['w_down'] = nrm((L, E, F, D), F ** -0.5)
    inp['b_down'] = nrm((L, E, D), 0.01)
    inp['final_norm_w'] = 1.0 + nrm((D,), 0.1)
    return inp


def reference(x_prompt, x_sample, cache_k, cache_v, state_ret, state_rwkv, c, c_ctx,
              norm1_w, norm2_w, mod_w, mod_b, w_in, ret_decay, rwkv_mu, rwkv_w0, rwkv_w_up,
              rwkv_a0, rwkv_a_up, rwkv_g_up, rwkv_k_k, rwkv_k_a, rwkv_r_k, rwkv_ln_w,
              q_norm_w, k_norm_w, w_branch, w_out, router_w, router_b, w_gate_up, b_gate_up,
              w_down, b_down, final_norm_w):
    yp, ys = x_prompt, x_sample
    ks_, vs_, srs, sws = [], [], [], []
    for l in range(DEPTH):
        p = {'norm1_w': norm1_w[l], 'norm2_w': norm2_w[l], 'mod_w': mod_w[l], 'mod_b': mod_b[l],
             'w_in': w_in[l], 'ret_decay': ret_decay[l], 'rwkv_mu': rwkv_mu[l], 'rwkv_w0': rwkv_w0[l],
             'rwkv_w_up': rwkv_w_up[l], 'rwkv_a0': rwkv_a0[l], 'rwkv_a_up': rwkv_a_up[l],
             'rwkv_g_up': rwkv_g_up[l], 'rwkv_k_k': rwkv_k_k[l], 'rwkv_k_a': rwkv_k_a[l],
             'rwkv_r_k': rwkv_r_k[l], 'rwkv_ln_w': rwkv_ln_w[l], 'q_norm_w': q_norm_w[l],
             'k_norm_w': k_norm_w[l], 'w_branch': w_branch[l], 'w_out': w_out[l],
             'router_w': router_w[l], 'router_b': router_b[l], 'w_gate_up': w_gate_up[l],
             'b_gate_up': b_gate_up[l], 'w_down': w_down[l], 'b_down': b_down[l]}
        yp, (k_l, v_l, sr_l, sw_l) = layer(yp, c_ctx, p, False, None)
        ks_.append(k_l)
        vs_.append(v_l)
        srs.append(sr_l)
        sws.append(sw_l)
        ctx = {'k': cache_k[:, l], 'v': cache_v[:, l],
               'state_ret': state_ret[:, l], 'state_rwkv': state_rwkv[:, l]}
        ys, _ = layer(ys, c, p, True, ctx)
    y_prompt = rms_norm(yp, final_norm_w)
    y_sample = rms_norm(ys, final_norm_w)
    new_cache_k = jnp.stack(ks_, axis=1)
    new_cache_v = jnp.stack(vs_, axis=1)
    new_state_ret = jnp.stack(srs, axis=1).astype(x_prompt.dtype)
    new_state_rwkv = jnp.stack(sws, axis=1).astype(x_prompt.dtype)
    return (y_prompt, y_sample, new_cache_k, new_cache_v, new_state_ret, new_state_rwkv)
```

```python
import functools

import jax
import jax.numpy as jnp
from jax import lax
from jax.experimental import pallas as pl
from jax.experimental.pallas import tpu as pltpu

F32 = jnp.float32
BF16 = jnp.bfloat16

D_MODEL = 1024
DEPTH = 2
GRID_W = 64
MIX_W = D_MODEL // 2
N_BRANCH = 3
EPS = 1e-6
HEAD_DIM = 128
ATT_HEADS = MIX_W // HEAD_DIM
ATT_KV_HEADS = ATT_HEADS // 2
ROPE_THETA = 10000.0
RET_HEADS = MIX_W // HEAD_DIM
RET_CHUNK = 128
RWKV_N = 64
RWKV_HEADS = MIX_W // RWKV_N
RWKV_CHUNK = 64
RWKV_LORA = 256
RWKV_LN_EPS = 64e-5
N_EXPERTS = 32
TOP_K = 4
D_FF = D_MODEL
SWIGLU_LIMIT = 7.0
SWIGLU_ALPHA = 1.702
RET_COLS = 4 * MIX_W
RWKV_COLS = 3 * MIX_W + RWKV_LORA
ATT_COLS = (ATT_HEADS + 2 * ATT_KV_HEADS) * HEAD_DIM
GATE_COLS = N_BRANCH * D_MODEL
MAIN_COLS = RET_COLS + RWKV_COLS + ATT_COLS
N_MOD = 6
MOD_ROWS = 8

LANE = 128
SUBLANE = 8
VMEM_LIMIT = 56 * 1024 * 1024

_NT = (((1,), (1,)), ((), ()))
_TN = (((0,), (0,)), ((), ()))


def _cparams(*sem):
    return pltpu.CompilerParams(dimension_semantics=sem, vmem_limit_bytes=VMEM_LIMIT)


def _bdot(a, b):
    return jnp.dot(a.astype(BF16), b.astype(BF16), preferred_element_type=F32)


def _bdot_g(a, b, dims):
    return lax.dot_general(a.astype(BF16), b.astype(BF16), dims, preferred_element_type=F32)


def _split3(x):
    hi = x.astype(BF16)
    r1 = x - hi.astype(F32)
    mid = r1.astype(BF16)
    lo = (r1 - mid.astype(F32)).astype(BF16)
    return hi, mid, lo


def _dot_x3(a, b):
    ah = a.astype(BF16)
    bh = b.astype(BF16)
    al = (a - ah.astype(F32)).astype(BF16)
    bl = (b - bh.astype(F32)).astype(BF16)
    d = functools.partial(jnp.dot, preferred_element_type=F32)
    return d(ah, bh) + (d(ah, bl) + d(al, bh))


def _dot_exact_rhs(a, b_exact):
    hi, mid, lo = _split3(a)
    d = functools.partial(jnp.dot, preferred_element_type=F32)
    return d(hi, b_exact) + (d(mid, b_exact) + d(lo, b_exact))


def _sigmoid(x):
    return 1.0 / (1.0 + jnp.exp(-x))


def _silu(x):
    return x * _sigmoid(x)


def _mod_kernel(cond_ref, w_ref, b_ref, o_ref):
    s = _silu(cond_ref[...])
    o_ref[...] = _dot_x3(s, w_ref[...]) + b_ref[...]


def modulation(cond_rows, mod_w, mod_b):
    n_layers, d, cols = mod_w.shape
    tn = d
    return pl.pallas_call(
        _mod_kernel,
        out_shape=jax.ShapeDtypeStruct((n_layers, MOD_ROWS, cols), F32),
        grid=(n_layers, cols // tn),
        in_specs=[
            pl.BlockSpec((MOD_ROWS, d), lambda l, j: (0, 0)),
            pl.BlockSpec((None, d, tn), lambda l, j: (l, 0, j)),
            pl.BlockSpec((None, 1, tn), lambda l, j: (l, 0, j)),
        ],
        out_specs=pl.BlockSpec((None, MOD_ROWS, tn), lambda l, j: (l, 0, j)),
        compiler_params=_cparams("arbitrary", "arbitrary"),
        name="modulation",
    )(cond_rows, mod_w, mod_b.reshape(n_layers, 1, cols))


def _mod_row(i, mod_base, tiles_per_seq):
    if mod_base == 0:
        return 0
    return mod_base + i // tiles_per_seq


def _modnorm(x, nw, shift, scale):
    ms = jnp.mean(x * x, axis=-1, keepdims=True)
    return (x * lax.rsqrt(ms + EPS)) * nw * (1.0 + scale) + shift


def _in_proj_kernel(x_ref, nw_ref, sh_ref, sc_ref, w_ref, o_ref, h_scr, *, mod_base, tiles_per_seq):
    i = pl.program_id(0)

    @pl.when(pl.program_id(1) == 0)
    def _():
        r = _mod_row(i, mod_base, tiles_per_seq)
        h = _modnorm(x_ref[...], nw_ref[...], sh_ref[pl.ds(r, 1), :], sc_ref[pl.ds(r, 1), :])
        h_scr[...] = h.astype(BF16)

    o_ref[...] = jnp.dot(h_scr[...], w_ref[...].astype(BF16), preferred_element_type=F32)


def in_proj(x, norm_w, mod, w_in, layer, col0, ncols, *, seq_len, mod_base):
    n, d = x.shape
    tm = min(n, 2048, seq_len if mod_base else n)
    tn = 256
    assert n % tm == 0 and ncols % tn == 0 and col0 % tn == 0 and (mod_base == 0 or seq_len % tm == 0)
    j0 = col0 // tn
    kern = functools.partial(_in_proj_kernel, mod_base=mod_base, tiles_per_seq=seq_len // tm)
    return pl.pallas_call(
        kern,
        out_shape=jax.ShapeDtypeStruct((n, ncols), F32),
        grid=(n // tm, ncols // tn),
        in_specs=[
            pl.BlockSpec((tm, d), lambda i, j: (i, 0)),
            pl.BlockSpec((None, 1, d), lambda i, j: (layer, 0, 0)),
            pl.BlockSpec((None, MOD_ROWS, d), lambda i, j: (layer, 0, 0)),
            pl.BlockSpec((None, MOD_ROWS, d), lambda i, j: (layer, 0, 1)),
            pl.BlockSpec((None, d, tn), lambda i, j: (layer, 0, j0 + j)),
        ],
        out_specs=pl.BlockSpec((tm, tn), lambda i, j: (i, j)),
        scratch_shapes=[pltpu.VMEM((tm, d), BF16)],
        compiler_params=_cparams("arbitrary", "arbitrary"),
        name="in_proj",
    )(x, norm_w, mod, mod, w_in)


def _rope(x, cos, sin_even, sin_odd):
    nxt = pltpu.roll(x, LANE - 1, axis=1)
    prv = pltpu.roll(x, 1, axis=1)
    return x * cos + nxt * sin_even + prv * sin_odd


def _rms(x, w):
    return x * lax.rsqrt(jnp.mean(x * x, axis=-1, keepdims=True) + EPS) * w


def _iota2(shape, dim):
    return lax.broadcasted_iota(jnp.int32, shape, dim)


def _retention_kernel(*refs, layer, latent, n_chunks):
    if latent:
        (decay_ref, q_ref, k_ref, v_ref, g_ref, cos_ref, se_ref, so_ref, s0_ref,
         o_ref, q_scr, k_scr, kf_scr, kb_scr) = refs
    else:
        (decay_ref, q_ref, k_ref, v_ref, g_ref,
         o_ref, sout_ref, q_scr, k_scr, kf_scr, kb_scr) = refs
    c_len = RET_CHUNK
    h = pl.program_id(1)
    lgf = -decay_ref[layer, 0, h]
    lgb = -decay_ref[layer, 1, h]

    if latent:
        q_scr[...] = _rope(q_ref[...], cos_ref[...], se_ref[...], so_ref[...])
        k_scr[...] = _rope(k_ref[...], cos_ref[...], se_ref[...], so_ref[...]) * (HEAD_DIM ** -0.5)
    else:
        q_scr[...] = q_ref[...]
        k_scr[...] = k_ref[...] * (HEAD_DIM ** -0.5)

    ii = _iota2((c_len, c_len), 0)
    jj = _iota2((c_len, c_len), 1)
    diff = (ii - jj).astype(F32)
    tok = ii.astype(F32)
    dcomb = (jnp.where(diff >= 0, jnp.exp(lgf * jnp.maximum(diff, 0.0)), 0.0)
             + jnp.where(diff <= 0, jnp.exp(lgb * jnp.maximum(-diff, 0.0)), 0.0))
    xi_f = jnp.exp(lgf * (tok + 1.0))
    xi_b = jnp.exp(lgb * (c_len - tok))
    zeta_f = jnp.exp(lgf * (c_len - 1.0 - tok))
    zeta_b = jnp.exp(lgb * tok)
    gc_f = jnp.exp(jnp.full((c_len, c_len), c_len, F32) * lgf)
    gc_b = jnp.exp(jnp.full((c_len, c_len), c_len, F32) * lgb)

    def rows(c):
        return pl.ds(pl.multiple_of(c * c_len, c_len), c_len)

    def local(c, carry):
        kc = k_scr[rows(c), :]
        vc = v_ref[rows(c), :]
        kf_scr[c] = _bdot_g(kc * zeta_f, vc, _TN)
        kb_scr[c] = _bdot_g(kc * zeta_b, vc, _TN)
        return carry

    lax.fori_loop(0, n_chunks, local, 0)

    if latent:
        sf0 = s0_ref[0]
        sb0 = s0_ref[1]
    else:
        sf0 = jnp.zeros((c_len, c_len), F32)
        sb0 = sf0

    def scan_f(c, s):
        kl = kf_scr[c]
        kf_scr[c] = s
        return s * gc_f + kl

    def scan_b(t, s):
        c = n_chunks - 1 - t
        kl = kb_scr[c]
        kb_scr[c] = s
        return s * gc_b + kl

    sf = lax.fori_loop(0, n_chunks, scan_f, sf0)
    sb = lax.fori_loop(0, n_chunks, scan_b, sb0)
    if not latent:
        sout_ref[0] = sf
        sout_ref[1] = sb

    def output(c, carry):
        qc = q_scr[rows(c), :]
        kc = k_scr[rows(c), :]
        vc = v_ref[rows(c), :]
        att = _bdot_g(qc, kc, _NT) * dcomb
        qx = jnp.concatenate([qc * xi_f, qc * xi_b], axis=1)
        ss = jnp.concatenate([kf_scr[c], kb_scr[c]], axis=0)
        o = _bdot(att, vc) + _bdot(qx, ss)
        mu = jnp.mean(o, axis=-1, keepdims=True)
        d = o - mu
        var = jnp.mean(d * d, axis=-1, keepdims=True)
        y = d * lax.rsqrt(var + EPS)
        o_ref[rows(c), :] = _silu(g_ref[rows(c), :]) * y
        return carry

    lax.fori_loop(0, n_chunks, output, 0)


def retention(z, ret_decay, layer, batch, seq_len, *, rope=None, state0=None):
    latent = rope is not None
    n = z.shape[0]
    t = seq_len
    n_chunks = t // RET_CHUNK
    hd = HEAD_DIM
    nh = RET_HEADS
    col = lambda base: pl.BlockSpec((t, hd), lambda b, h, base=base: (b, base + h))
    in_specs = [pl.BlockSpec(memory_space=pltpu.SMEM), col(0), col(nh), col(2 * nh), col(3 * nh)]
    args = [ret_decay, z, z, z, z]
    out_shape = [jax.ShapeDtypeStruct((n, MIX_W), F32)]
    out_specs = [pl.BlockSpec((t, hd), lambda b, h: (b, h))]
    if latent:
        tab = pl.BlockSpec((t, hd), lambda b, h: (0, 0))
        in_specs += [tab, tab, tab,
                     pl.BlockSpec((None, None, 2, None, hd, hd), lambda b, h: (b, layer, 0, h, 0, 0))]
        args += [*rope, state0]
    else:
        out_shape.append(jax.ShapeDtypeStruct((batch, 2, nh, hd, hd), F32))
        out_specs.append(pl.BlockSpec((None, 2, None, hd, hd), lambda b, h: (b, 0, h, 0, 0)))
    kern = functools.partial(_retention_kernel, layer=layer, latent=latent, n_chunks=n_chunks)
    res = pl.pallas_call(
        kern,
        out_shape=out_shape,
        grid=(batch, nh),
        in_specs=in_specs,
        out_specs=out_specs,
        scratch_shapes=[pltpu.VMEM((t, hd), F32), pltpu.VMEM((t, hd), F32),
                        pltpu.VMEM((n_chunks, hd, hd), F32), pltpu.VMEM((n_chunks, hd, hd), F32)],
        compiler_params=_cparams("arbitrary", "arbitrary"),
        name="retention",
    )(*args)
    return res if not latent else res[0]


def _attention_kernel(*refs, latent, tq, past):
    if latent:
        (q_ref, k_ref, v_ref, qn_ref, kn_ref, cos_ref, se_ref, so_ref, ck_ref, cv_ref,
         o_ref, kb_scr, vb_scr) = refs
    else:
        (q_ref, k_ref, v_ref, qn_ref, kn_ref,
         o_ref, nk_ref, nv_ref, kb_scr, vb_scr) = refs
    qi = pl.program_id(2)
    hd = HEAD_DIM

    @pl.when(qi == 0)
    def _():
        kn = _rms(k_ref[...], kn_ref[...])
        if latent:
            kb_scr[0:past, :] = ck_ref[...].astype(BF16)
            vb_scr[0:past, :] = cv_ref[...].astype(BF16)
            kb_scr[past:, :] = _rope(kn, cos_ref[...], se_ref[...], so_ref[...]).astype(BF16)
            vb_scr[past:, :] = v_ref[...].astype(BF16)
        else:
            kb_scr[...] = kn.astype(BF16)
            vb_scr[...] = v_ref[...].astype(BF16)
            nk_ref[...] = kn
            nv_ref[...] = v_ref[...]

    q = q_ref[...]
    heads = []
    for g in range(2):
        qg = _rms(q[:, g * hd:(g + 1) * hd], qn_ref[...])
        if latent:
            r = pl.ds(pl.multiple_of(qi * tq, tq), tq)
            qg = _rope(qg, cos_ref[r, :], se_ref[r, :], so_ref[r, :])
        heads.append(qg)
    qq = jnp.concatenate(heads, axis=0)
    s = _bdot_g(qq, kb_scr[...], _NT) * (hd ** -0.5)
    m = jnp.max(s, axis=-1, keepdims=True)
    p = jnp.exp(s - m)
    l = jnp.sum(p, axis=-1, keepdims=True)
    o = jnp.dot(p.astype(BF16), vb_scr[...], preferred_element_type=F32) / l
    o_ref[...] = jnp.concatenate([o[:tq], o[tq:]], axis=1)


def attention(z, q_norm_w, k_norm_w, layer, batch, seq_len, *, rope=None, cache_k=None, cache_v=None):
    latent = rope is not None
    n = z.shape[0]
    t = seq_len
    hd = HEAD_DIM
    tq = min(t, 256)
    past = cache_k.shape[2] if latent else 0
    q0 = (RET_COLS + RWKV_COLS) // (2 * hd)
    k0 = (RET_COLS + RWKV_COLS) // hd + ATT_HEADS
    v0 = k0 + ATT_KV_HEADS
    nq = t // tq
    in_specs = [
        pl.BlockSpec((tq, 2 * hd), lambda b, g, i: (b * nq + i, q0 + g)),
        pl.BlockSpec((t, hd), lambda b, g, i: (b, k0 + g)),
        pl.BlockSpec((t, hd), lambda b, g, i: (b, v0 + g)),
        pl.BlockSpec((None, 1, hd), lambda b, g, i: (layer, 0, 0)),
        pl.BlockSpec((None, 1, hd), lambda b, g, i: (layer, 0, 0)),
    ]
    args = [z, z, z, q_norm_w, k_norm_w]
    out_shape = [jax.ShapeDtypeStruct((n, MIX_W), F32)]
    out_specs = [pl.BlockSpec((tq, 2 * hd), lambda b, g, i: (b * nq + i, g))]
    if latent:
        tab = pl.BlockSpec((t, hd), lambda b, g, i: (0, 0))
        cache = pl.BlockSpec((None, None, past, hd), lambda b, g, i: (b, layer, 0, g))
        in_specs += [tab, tab, tab, cache, cache]
        flat = cache_k.shape[:3] + (ATT_KV_HEADS * hd,)
        args += [*rope, cache_k.reshape(flat), cache_v.reshape(flat)]
    else:
        new = jax.ShapeDtypeStruct((batch, t, ATT_KV_HEADS * hd), F32)
        out_shape += [new, new]
        spec = pl.BlockSpec((None, t, hd), lambda b, g, i: (b, 0, g))
        out_specs += [spec, spec]
    kern = functools.partial(_attention_kernel, latent=latent, tq=tq, past=past)
    res = pl.pallas_call(
        kern,
        out_shape=out_shape,
        grid=(batch, ATT_KV_HEADS, nq),
        in_specs=in_specs,
        out_specs=out_specs,
        scratch_shapes=[pltpu.VMEM((past + t, hd), BF16), pltpu.VMEM((past + t, hd), BF16)],
        compiler_params=_cparams("arbitrary", "arbitrary", "arbitrary"),
        name="attention",
    )(*args)
    return res if not latent else res[0]


def _softplus(x):
    return jnp.maximum(x, 0.0) + jnp.log1p(jnp.exp(-jnp.abs(x)))


def _token_shift(u, prev_row, next_row, mu):
    tr = u.shape[0]
    row = _iota2(u.shape, 0)
    prev = jnp.where(row == 0, prev_row, pltpu.roll(u, 1, axis=0))
    nxt = jnp.where(row == tr - 1, next_row, pltpu.roll(u, tr - 1, axis=0))
    return u + mu[0:1, :] * (prev - u) + mu[1:2, :] * (nxt - u)


def _rwkv_prep_kernel(zr, zk, zv, zl, pr, pk, pv, pl_, nr, nk, nv, nl, mu_r, mu_k, mu_v, mu_l,
                      w0_ref, wup_ref, a0_ref, aup_ref, gup_ref, kkw_ref, kaw_ref, rkw_ref, seg_ref,
                      r_o, k_o, v_o, kk_o, b_o, lw0_o, lw1_o, bonus_o, gate_o, *, tiles_per_seq):
    i = pl.program_id(0)
    pos = i % tiles_per_seq
    keep_prev = jnp.where(pos == 0, 0.0, 1.0)
    keep_next = jnp.where(pos == tiles_per_seq - 1, 0.0, 1.0)

    def shifted(cur, prv, nxt, mu):
        return _token_shift(cur[...], prv[SUBLANE - 1:SUBLANE, :] * keep_prev, nxt[0:1, :] * keep_next, mu[...])

    wr = shifted(zr, pr, nr, mu_r)
    wk = shifted(zk, pk, nk, mu_k)
    wv = shifted(zv, pv, nv, mu_v)
    lo = shifted(zl, pl_, nl, mu_l)

    seg = seg_ref[...]
    icl = _sigmoid(a0_ref[...] + _dot_x3(lo, aup_ref[...]))
    gate_o[...] = _dot_x3(_sigmoid(lo), gup_ref[...])
    th = jnp.tanh(lo)
    for d, lw_o in enumerate((lw0_o, lw1_o)):
        wl = w0_ref[d:d + 1, :] + _dot_x3(th, wup_ref[d])
        lw_o[...] = -jnp.exp(-_softplus(-wl) - 0.5)
    kk = wk * kkw_ref[...]
    kk = kk * lax.rsqrt(_dot_exact_rhs(kk * kk, seg) + 1e-12)
    k = wk * (1.0 + (icl - 1.0) * kaw_ref[...])
    r_o[...] = wr
    k_o[...] = k
    v_o[...] = wv
    kk_o[...] = kk
    b_o[...] = kk * icl
    bonus_o[...] = _dot_exact_rhs(wr * k * rkw_ref[...], seg) * wv


def rwkv_prep(z, p, layer, seq_len):
    n = z.shape[0]
    tr = 256
    w = MIX_W
    c0 = RET_COLS // w
    cl = (RET_COLS + 3 * w) // RWKV_LORA
    nt = n // tr
    rb = tr // SUBLANE
    last8 = n // SUBLANE - 1
    cur = lambda width, c: pl.BlockSpec((tr, width), lambda i, c=c: (i, c))
    prv = lambda width, c: pl.BlockSpec((SUBLANE, width), lambda i, c=c: (jnp.maximum(i * rb - 1, 0), c))
    nxt = lambda width, c: pl.BlockSpec((SUBLANE, width), lambda i, c=c: (jnp.minimum((i + 1) * rb, last8), c))
    cols = [(w, c0), (w, c0 + 1), (w, c0 + 2), (RWKV_LORA, cl)]
    mu_cols = [(w, 0), (w, 1), (w, 2), (RWKV_LORA, 3 * w // RWKV_LORA)]
    whole = lambda *shape: pl.BlockSpec((None,) + shape, lambda i: (layer,) + (0,) * len(shape))
    in_specs = ([cur(*c) for c in cols] + [prv(*c) for c in cols] + [nxt(*c) for c in cols]
                + [pl.BlockSpec((None, 2, width), lambda i, c=c: (layer, 0, c)) for width, c in mu_cols]
                + [whole(2, w), whole(2, RWKV_LORA, w), whole(1, w), whole(RWKV_LORA, w), whole(RWKV_LORA, w),
                   whole(1, w), whole(1, w), whole(1, w), pl.BlockSpec((w, w), lambda i: (0, 0))])
    out = jax.ShapeDtypeStruct((n, w), F32)
    kern = functools.partial(_rwkv_prep_kernel, tiles_per_seq=seq_len // tr)
    return pl.pallas_call(
        kern,
        out_shape=[out] * 9,
        grid=(nt,),
        in_specs=in_specs,
        out_specs=[pl.BlockSpec((tr, w), lambda i: (i, 0))] * 9,
        compiler_params=_cparams("arbitrary"),
        name="rwkv_prep",
    )(*([z] * 12), *([p["rwkv_mu"]] * 4), p["rwkv_w0"], p["w_up_pad"], p["rwkv_a0"], p["a_up_pad"], p["g_up_pad"],
      p["rwkv_k_k"], p["rwkv_k_a"], p["rwkv_r_k"], p["seg_ones"])


def _tri_inverse(a, eye, blk16, off32, off64):
    d = jnp.where(blk16, a, 0.0)
    t = eye + d
    pw = d
    for _ in range(3):
        pw = _bdot(pw, pw)
        t = _bdot(t, eye + pw)
    for off in (off32, off64):
        t = t + _bdot(_bdot(t, jnp.where(off, a, 0.0)), t)
    return t


def _rwkv_scan_kernel(*refs, latent, n_chunks):
    if latent:
        (r_ref, k_ref, v_ref, kk_ref, b_ref, lw0_ref, lw1_ref, bonus_ref, gate_ref, lnw_ref, s0_ref,
         o_ref, yf_scr, yb_scr, s_scr) = refs
    else:
        (r_ref, k_ref, v_ref, kk_ref, b_ref, lw0_ref, lw1_ref, bonus_ref, gate_ref, lnw_ref,
         o_ref, sout_ref, yf_scr, yb_scr, s_scr) = refs
    cl = RWKV_CHUNK
    hn = RWKV_N
    two = 2 * cl
    m0 = _iota2((1, LANE), 1) < hn
    row = _iota2((two, two), 0)
    col = _iota2((two, two), 1)
    same = (row >= cl) == (col >= cl)
    eye = jnp.where(row == col, 1.0, 0.0)
    blk16 = (row >> 4) == (col >> 4)
    off32 = ((row >> 5) == (col >> 5)) & ~blk16
    off64 = same & ((row >> 5) != (col >> 5))
    r64 = _iota2((cl, cl), 0)
    c64 = _iota2((cl, cl), 1)
    masks = (
        (same & (row >= col), same & (row > col), jnp.where(r64 >= c64, 1.0, 0.0).astype(BF16)),
        (same & (row <= col), same & (row < col), jnp.where(r64 <= c64, 1.0, 0.0).astype(BF16)),
    )

    def stack(x):
        return jnp.concatenate([jnp.where(m0, x, 0.0), jnp.where(m0, 0.0, x)], axis=0)

    zero = jnp.zeros((hn, hn), F32)
    for d in range(2):
        if latent:
            s_scr[d] = jnp.concatenate(
                [jnp.concatenate([s0_ref[d, 0], zero], axis=1), jnp.concatenate([zero, s0_ref[d, 1]], axis=1)], axis=0)
        else:
            s_scr[d] = jnp.zeros((LANE, LANE), F32)

    def chunk(d, ci, lw_ref, y_scr):
        incl, strict, tri = masks[d]
        rows = pl.ds(pl.multiple_of(ci * cl, cl), cl)
        lw = lw_ref[rows, :]
        hi, mid, lo = _split3(lw)
        dot = functools.partial(jnp.dot, preferred_element_type=F32)
        lg = dot(tri, hi) + (dot(tri, mid) + dot(tri, lo))
        tot = lg[cl - 1:cl, :] if d == 0 else lg[0:1, :]
        e_in = jnp.exp(lg)
        e_neg = jnp.exp(-lg)
        e_out = jnp.exp(tot - lg)
        kk = kk_ref[rows, :]
        bb = b_ref[rows, :]
        kc = k_ref[rows, :]
        a_t = stack(-kk * jnp.exp(lg - lw))
        r_t = stack(r_ref[rows, :] * e_in)
        ar = jnp.concatenate([a_t, r_t], axis=0)
        bk = jnp.concatenate([stack(bb * e_neg), stack(kc * e_neg)], axis=0)
        cross = _bdot_g(ar, bk, _NT)
        a_ab = jnp.where(strict, cross[:two, :two], 0.0)
        a_ak = jnp.where(strict, cross[:two, two:], 0.0)
        m_rb = jnp.where(incl, cross[two:, :two], 0.0)
        m_rk = jnp.where(incl, cross[two:, two:], 0.0)
        t_inv = _tri_inverse(a_ab, eye, blk16, off32, off64)
        s = s_scr[d]
        ars = _bdot_g(ar, s, _NT)
        v_st = stack(v_ref[rows, :])
        u = _bdot(t_inv, ars[:two] + _bdot(a_ak, v_st))
        uv = jnp.concatenate([u, v_st], axis=0)
        y = ars[two:] + _bdot(jnp.concatenate([m_rb, m_rk], axis=1), uv)
        bk_out = jnp.concatenate([stack(bb * e_out), stack(kc * e_out)], axis=0)
        s_scr[d] = s * jnp.exp(tot) + _bdot_g(uv, bk_out, _TN)
        y_scr[rows, :] = y[:cl] + y[cl:]

    def body(c, carry):
        chunk(0, c, lw0_ref, yf_scr)
        chunk(1, n_chunks - 1 - c, lw1_ref, yb_scr)
        return carry

    lax.fori_loop(0, n_chunks, body, 0)

    if not latent:
        for d in range(2):
            s = s_scr[d]
            sout_ref[d, 0] = s[:hn, :hn]
            sout_ref[d, 1] = s[hn:, hn:]

    rt = 256 if (n_chunks * cl) % 256 == 0 else n_chunks * cl

    def epilogue(j, carry):
        rows = pl.ds(pl.multiple_of(j * rt, rt), rt)
        y = yf_scr[rows, :] + yb_scr[rows, :]
        s_all = jnp.sum(y, axis=-1, keepdims=True)
        s_0 = jnp.sum(jnp.where(m0, y, 0.0), axis=-1, keepdims=True)
        mu = jnp.where(m0, s_0, s_all - s_0) * (1.0 / hn)
        dlt = y - mu
        q_all = jnp.sum(dlt * dlt, axis=-1, keepdims=True)
        q_0 = jnp.sum(jnp.where(m0, dlt * dlt, 0.0), axis=-1, keepdims=True)
        var = jnp.where(m0, q_0, q_all - q_0) * (1.0 / hn)
        yn = dlt * lax.rsqrt(var + RWKV_LN_EPS) * lnw_ref[...] + bonus_ref[rows, :]
        o_ref[rows, :] = yn * gate_ref[rows, :]
        return carry

    lax.fori_loop(0, (n_chunks * cl) // rt, epilogue, 0)


def rwkv_scan(ops, ln_w, layer, batch, seq_len, *, state0=None):
    latent = state0 is not None
    n = ops[0].shape[0]
    t = seq_len
    hn = RWKV_N
    npair = RWKV_HEADS // 2
    blk = pl.BlockSpec((t, LANE), lambda b, h: (b, h))
    in_specs = [blk] * 9 + [pl.BlockSpec((None, 1, LANE), lambda b, h: (layer, 0, h))]
    args = [*ops, ln_w]
    out_shape = [jax.ShapeDtypeStruct((n, MIX_W), F32)]
    out_specs = [blk]
    if latent:
        in_specs.append(pl.BlockSpec((None, None, 2, 2, hn, hn), lambda b, h: (b, layer, 0, h, 0, 0)))
        args.append(state0)
    else:
        out_shape.append(jax.ShapeDtypeStruct((batch, 2, RWKV_HEADS, hn, hn), F32))
        out_specs.append(pl.BlockSpec((None, 2, 2, hn, hn), lambda b, h: (b, 0, h, 0, 0)))
    kern = functools.partial(_rwkv_scan_kernel, latent=latent, n_chunks=t // RWKV_CHUNK)
    res = pl.pallas_call(
        kern,
        out_shape=out_shape,
        grid=(batch, npair),
        in_specs=in_specs,
        out_specs=out_specs,
        scratch_shapes=[pltpu.VMEM((t, LANE), F32), pltpu.VMEM((t, LANE), F32), pltpu.VMEM((2, LANE, LANE), F32)],
        compiler_params=_cparams("arbitrary", "arbitrary"),
        name="rwkv_scan",
    )(*args)
    return res if not latent else res[0]


def _topk_softmax(logits):
    lane = _iota2(logits.shape, 1)
    out_lane = _iota2((logits.shape[0], TOP_K), 1)
    n_exp = logits.shape[1]
    vals, idxs = [], []
    work = logits
    for _ in range(TOP_K):
        m = jnp.max(work, axis=-1, keepdims=True)
        idx = jnp.min(jnp.where(work == m, lane, n_exp), axis=-1, keepdims=True)
        vals.append(m)
        idxs.append(idx)
        work = jnp.where(lane == idx, -jnp.inf, work)
    es = [jnp.exp(v - vals[0]) for v in vals]
    denom = es[0] + es[1] + es[2] + es[3]
    top_i = jnp.zeros((logits.shape[0], TOP_K), jnp.int32)
    top_p = jnp.zeros((logits.shape[0], TOP_K), F32)
    for j in range(TOP_K):
        top_i = jnp.where(out_lane == j, idxs[j], top_i)
        top_p = jnp.where(out_lane == j, es[j] / denom, top_p)
    return top_i, top_p


def _merge_kernel(x_ref, oret_ref, orwkv_ref, oatt_ref, zg_ref, wb_ref, wo_ref, g1_ref, nw2_ref, sh2_ref, sc2_ref,
                  rw_ref, rb_ref, x1_ref, h2_ref, ti_ref, tp_ref, *, mod_base, tiles_per_seq):
    r = _mod_row(pl.program_id(0), mod_base, tiles_per_seq)
    d = x_ref.shape[1]
    acc = None
    for nb, o_ref in enumerate((oret_ref, orwkv_ref, oatt_ref)):
        proj = jnp.dot(o_ref[...].astype(BF16), wb_ref[nb], preferred_element_type=F32)
        term = _sigmoid(zg_ref[:, nb * d:(nb + 1) * d]) * proj
        acc = term if acc is None else acc + term
    out = jnp.dot(acc.astype(BF16), wo_ref[...], preferred_element_type=F32)
    x1 = x_ref[...] + g1_ref[pl.ds(r, 1), :] * out
    x1_ref[...] = x1
    h2 = _modnorm(x1, nw2_ref[...], sh2_ref[pl.ds(r, 1), :], sc2_ref[pl.ds(r, 1), :])
    h2_ref[...] = h2.astype(BF16)
    logits = _dot_x3(h2, rw_ref[...]) + rb_ref[...]
    top_i, top_p = _topk_softmax(logits)
    ti_ref[...] = top_i
    tp_ref[...] = top_p


def merge(x, o_ret, o_rwkv, o_att, zg, p, mod, layer, *, seq_len, mod_base):
    n, d = x.shape
    tm = 256
    w = MIX_W
    row = lambda width: pl.BlockSpec((tm, width), lambda i: (i, 0))
    modc = lambda c: pl.BlockSpec((None, MOD_ROWS, d), lambda i, c=c: (layer, 0, c))
    kern = functools.partial(_merge_kernel, mod_base=mod_base, tiles_per_seq=seq_len // tm)
    return pl.pallas_call(
        kern,
        out_shape=[jax.ShapeDtypeStruct((n, d), F32), jax.ShapeDtypeStruct((n, d), BF16),
                   jax.ShapeDtypeStruct((n, TOP_K), jnp.int32), jax.ShapeDtypeStruct((n, TOP_K), F32)],
        grid=(n // tm,),
        in_specs=[
            row(d), row(w), row(w), row(w), row(N_BRANCH * d),
            pl.BlockSpec((None, N_BRANCH, w, d), lambda i: (layer, 0, 0, 0)),
            pl.BlockSpec((None, d, d), lambda i: (layer, 0, 0)),
            modc(2),
            pl.BlockSpec((None, 1, d), lambda i: (layer, 0, 0)),
            modc(3), modc(4),
            pl.BlockSpec((None, d, N_EXPERTS), lambda i: (layer, 0, 0)),
            pl.BlockSpec((None, 1, N_EXPERTS), lambda i: (layer, 0, 0)),
        ],
        out_specs=[row(d), row(d), row(TOP_K), row(TOP_K)],
        compiler_params=_cparams("arbitrary"),
        name="merge",
    )(x, o_ret, o_rwkv, o_att, zg, p["w_branch_bf"], p["w_out_bf"], mod, p["norm2_w"], mod, mod,
      p["router_w"], p["router_b"])


def _moe_kernel(te_ref, xs_ref, gate_ref, wgu_ref, bgu_ref, wdn_ref, bdn_ref, y_ref, wgu_scr, wdn_scr):
    i = pl.program_id(0)
    prev = te_ref[jnp.maximum(i - 1, 0)]

    @pl.when((i == 0) | (te_ref[i] != prev))
    def _():
        wgu_scr[...] = wgu_ref[...].astype(BF16)
        wdn_scr[...] = wdn_ref[...].astype(BF16)

    f = wdn_scr.shape[0]
    gu = jnp.dot(xs_ref[...], wgu_scr[...], preferred_element_type=F32) + bgu_ref[...]
    gate = jnp.minimum(gu[:, :f], SWIGLU_LIMIT)
    up = jnp.clip(gu[:, f:], -SWIGLU_LIMIT, SWIGLU_LIMIT)
    act = (up + 1.0) * gate * _sigmoid(SWIGLU_ALPHA * gate)
    y = jnp.dot(act.astype(BF16), wdn_scr[...], preferred_element_type=F32) + bdn_ref[...]
    y_ref[...] = gate_ref[...] * y


def moe_experts(xs, slot_gate, tile_expert, p, layer, tile):
    s, d = xs.shape
    f = D_FF
    grid_spec = pltpu.PrefetchScalarGridSpec(
        num_scalar_prefetch=1,
        grid=(s // tile,),
        in_specs=[
            pl.BlockSpec((tile, d), lambda i, te: (i, 0)),
            pl.BlockSpec((tile, 1), lambda i, te: (i, 0)),
            pl.BlockSpec((None, None, d, 2 * f), lambda i, te: (layer, te[i], 0, 0)),
            pl.BlockSpec((None, None, 1, 2 * f), lambda i, te: (layer, te[i], 0, 0)),
            pl.BlockSpec((None, None, f, d), lambda i, te: (layer, te[i], 0, 0)),
            pl.BlockSpec((None, None, 1, d), lambda i, te: (layer, te[i], 0, 0)),
        ],
        out_specs=pl.BlockSpec((tile, d), lambda i, te: (i, 0)),
        scratch_shapes=[pltpu.VMEM((d, 2 * f), BF16), pltpu.VMEM((f, d), BF16)],
    )
    return pl.pallas_call(
        _moe_kernel,
        out_shape=jax.ShapeDtypeStruct((s, d), F32),
        grid_spec=grid_spec,
        compiler_params=_cparams("arbitrary"),
        name="moe_experts",
    )(tile_expert, xs, slot_gate, p["w_gate_up"], p["b_gate_up"], p["w_down"], p["b_down"])


def _residual_kernel(x1_ref, moe_ref, g2_ref, fw_ref, o_ref, *, mod_base, tiles_per_seq, final):
    r = _mod_row(pl.program_id(0), mod_base, tiles_per_seq)
    x2 = x1_ref[...] + g2_ref[pl.ds(r, 1), :] * moe_ref[...]
    o_ref[...] = _rms(x2, fw_ref[...]) if final else x2


def residual(x1, moe_out, row0, mod, final_w, layer, *, seq_len, mod_base, final):
    n, d = x1.shape
    tm = 256
    assert row0 % tm == 0
    r0 = row0 // tm
    kern = functools.partial(_residual_kernel, mod_base=mod_base, tiles_per_seq=seq_len // tm, final=final)
    return pl.pallas_call(
        kern,
        out_shape=jax.ShapeDtypeStruct((n, d), F32),
        grid=(n // tm,),
        in_specs=[
            pl.BlockSpec((tm, d), lambda i: (i, 0)),
            pl.BlockSpec((tm, d), lambda i: (r0 + i, 0)),
            pl.BlockSpec((None, MOD_ROWS, d), lambda i: (layer, 0, 5)),
            pl.BlockSpec((1, d), lambda i: (0, 0)),
        ],
        out_specs=pl.BlockSpec((tm, d), lambda i: (i, 0)),
        compiler_params=_cparams("arbitrary"),
        name="residual",
    )(x1, moe_out, mod, final_w)


MOE_TILE = 256


def _rope_tables(t):
    rows = t // GRID_W
    row = jnp.repeat(jnp.arange(rows, dtype=F32), GRID_W)
    col = jnp.tile(jnp.arange(GRID_W, dtype=F32), rows)
    n_freq = HEAD_DIM // 4
    inv = ROPE_THETA ** (-jnp.arange(n_freq, dtype=F32) / n_freq)
    ang = jnp.concatenate([row[:, None] * inv, col[:, None] * inv], axis=-1)
    cos = jnp.repeat(jnp.cos(ang), 2, axis=-1)
    sin = jnp.repeat(jnp.sin(ang), 2, axis=-1)
    even = (jnp.arange(HEAD_DIM) % 2 == 0)[None, :]
    return cos, jnp.where(even, -sin, 0.0), jnp.where(even, 0.0, sin)


def _dispatch(top_i, top_p, tile):
    n = top_i.shape[0]
    pairs = n * TOP_K
    e = top_i.reshape(pairs)
    onehot = (e[:, None] == jnp.arange(N_EXPERTS, dtype=jnp.int32)[None, :]).astype(jnp.int32)
    counts = jnp.sum(onehot, axis=0)
    rank = jnp.take_along_axis(jnp.cumsum(onehot, axis=0) - onehot, e[:, None], axis=1)[:, 0]
    padded = ((counts + tile - 1) // tile) * tile
    ends = jnp.cumsum(padded)
    pos = (ends - padded)[e] + rank
    n_slots = pairs + N_EXPERTS * tile
    slot_token = jnp.zeros((n_slots,), jnp.int32).at[pos].set(jnp.arange(pairs, dtype=jnp.int32) // TOP_K)
    slot_gate = jnp.zeros((n_slots,), F32).at[pos].set(top_p.reshape(pairs))
    tile_start = jnp.arange(n_slots // tile, dtype=jnp.int32) * tile
    tile_expert = jnp.minimum(jnp.searchsorted(ends, tile_start, side="right"), N_EXPERTS - 1).astype(jnp.int32)
    return pos.reshape(n, TOP_K), slot_token, slot_gate.reshape(n_slots, 1), tile_expert


def _layer_params(a, n_layers):
    d = D_MODEL
    w = MIX_W
    p = dict(a)
    for name in ("norm1_w", "norm2_w"):
        p[name] = a[name].reshape(n_layers, 1, d)
    for name in ("rwkv_a0", "rwkv_k_k", "rwkv_k_a", "rwkv_ln_w"):
        p[name] = a[name].reshape(n_layers, 1, w)
    p["rwkv_r_k"] = a["rwkv_r_k"].reshape(n_layers, 1, w)
    p["q_norm_w"] = a["q_norm_w"].reshape(n_layers, 1, HEAD_DIM)
    p["k_norm_w"] = a["k_norm_w"].reshape(n_layers, 1, HEAD_DIM)
    wr, ar, gr = a["rwkv_w_up"].shape[2], a["rwkv_a_up"].shape[1], a["rwkv_g_up"].shape[1]
    assert wr + ar + gr == RWKV_LORA
    p["w_up_pad"] = jnp.pad(a["rwkv_w_up"], ((0, 0), (0, 0), (0, ar + gr), (0, 0)))
    p["a_up_pad"] = jnp.pad(a["rwkv_a_up"], ((0, 0), (wr, gr), (0, 0)))
    p["g_up_pad"] = jnp.pad(a["rwkv_g_up"], ((0, 0), (wr + ar, 0), (0, 0)))
    head = jnp.arange(w, dtype=jnp.int32) // RWKV_N
    p["seg_ones"] = (head[:, None] == head[None, :]).astype(BF16)
    p["w_branch_bf"] = a["w_branch"].astype(BF16)
    p["w_out_bf"] = a["w_out"].astype(BF16)
    p["router_b"] = a["router_b"].reshape(n_layers, 1, N_EXPERTS)
    p["b_gate_up"] = a["b_gate_up"].reshape(n_layers, N_EXPERTS, 1, 2 * D_FF)
    p["b_down"] = a["b_down"].reshape(n_layers, N_EXPERTS, 1, d)
    return p


def _mixer(x, p, mod, layer, batch, seq_len, mod_base, latent_inputs):
    latent = latent_inputs is not None
    z = in_proj(x, p["norm1_w"], mod, p["w_in"], layer, 0, MAIN_COLS, seq_len=seq_len, mod_base=mod_base)
    zg = in_proj(x, p["norm1_w"], mod, p["w_in"], layer, MAIN_COLS, GATE_COLS, seq_len=seq_len, mod_base=mod_base)
    ops = rwkv_prep(z, p, layer, seq_len)
    if latent:
        rope, cache_k, cache_v, state_ret, state_rwkv = latent_inputs
        o_ret = retention(z, p["ret_decay"], layer, batch, seq_len, rope=rope, state0=state_ret)
        o_rwkv = rwkv_scan(ops, p["rwkv_ln_w"], layer, batch, seq_len, state0=state_rwkv)
        o_att = attention(z, p["q_norm_w"], p["k_norm_w"], layer, batch, seq_len,
                          rope=rope, cache_k=cache_k, cache_v=cache_v)
        ctx_out = None
    else:
        o_ret, s_ret = retention(z, p["ret_decay"], layer, batch, seq_len)
        o_rwkv, s_rwkv = rwkv_scan(ops, p["rwkv_ln_w"], layer, batch, seq_len)
        o_att, new_k, new_v = attention(z, p["q_norm_w"], p["k_norm_w"], layer, batch, seq_len)
        ctx_out = (new_k, new_v, s_ret, s_rwkv)
    merged = merge(x, o_ret, o_rwkv, o_att, zg, p, mod, layer, seq_len=seq_len, mod_base=mod_base)
    return merged, ctx_out


def kernel(x_prompt, x_sample, cache_k, cache_v, state_ret, state_rwkv, c, c_ctx, norm1_w, norm2_w, mod_w, mod_b, w_in, ret_decay, rwkv_mu, rwkv_w0, rwkv_w_up, rwkv_a0, rwkv_a_up, rwkv_g_up, rwkv_k_k, rwkv_k_a, rwkv_r_k, rwkv_ln_w, q_norm_w, k_norm_w, w_branch, w_out, router_w, router_b, w_gate_up, b_gate_up, w_down, b_down, final_norm_w):
    bp, tp, d = x_prompt.shape
    bs, ts, _ = x_sample.shape
    n_layers = w_in.shape[0]
    assert bs + 1 <= MOD_ROWS
    p = _layer_params(dict(
        norm1_w=norm1_w, norm2_w=norm2_w, w_in=w_in, ret_decay=ret_decay, rwkv_mu=rwkv_mu, rwkv_w0=rwkv_w0,
        rwkv_w_up=rwkv_w_up, rwkv_a0=rwkv_a0, rwkv_a_up=rwkv_a_up, rwkv_g_up=rwkv_g_up, rwkv_k_k=rwkv_k_k,
        rwkv_k_a=rwkv_k_a, rwkv_r_k=rwkv_r_k, rwkv_ln_w=rwkv_ln_w, q_norm_w=q_norm_w, k_norm_w=k_norm_w,
        w_branch=w_branch, w_out=w_out, router_w=router_w, router_b=router_b, w_gate_up=w_gate_up,
        b_gate_up=b_gate_up, w_down=w_down, b_down=b_down), n_layers)
    cond_rows = jnp.zeros((MOD_ROWS, d), F32).at[0].set(c_ctx).at[1:1 + bs].set(c)
    mod = modulation(cond_rows, mod_w, mod_b)
    rope = _rope_tables(ts)
    final_w = final_norm_w.reshape(1, d)
    np_, ns = bp * tp, bs * ts

    xp = x_prompt.reshape(np_, d)
    xs = x_sample.reshape(ns, d)
    ks_, vs_, srs, sws = [], [], [], []
    for layer in range(n_layers):
        (x1p, h2p, tip, tpp), (k_l, v_l, sr_l, sw_l) = _mixer(xp, p, mod, layer, bp, tp, 0, None)
        (x1s, h2s, tis, tps), _ = _mixer(xs, p, mod, layer, bs, ts, 1,
                                         (rope, cache_k, cache_v, state_ret, state_rwkv))
        ks_.append(k_l.reshape(bp, tp, ATT_KV_HEADS, HEAD_DIM))
        vs_.append(v_l.reshape(bp, tp, ATT_KV_HEADS, HEAD_DIM))
        srs.append(sr_l)
        sws.append(sw_l)
        h2 = jnp.concatenate([h2p, h2s], axis=0)
        top_i = jnp.concatenate([tip, tis], axis=0)
        top_p = jnp.concatenate([tpp, tps], axis=0)
        pos, slot_token, slot_gate, tile_expert = _dispatch(top_i, top_p, MOE_TILE)
        ys = moe_experts(jnp.take(h2, slot_token, axis=0), slot_gate, tile_expert, p, layer, MOE_TILE)
        moe_out = jnp.sum(jnp.take(ys, pos.reshape(-1), axis=0).reshape(np_ + ns, TOP_K, d), axis=1)
        final = layer == n_layers - 1
        xp = residual(x1p, moe_out, 0, mod, final_w, layer, seq_len=tp, mod_base=0, final=final)
        xs = residual(x1s, moe_out, np_, mod, final_w, layer, seq_len=ts, mod_base=1, final=final)
    return (xp.reshape(bp, tp, d), xs.reshape(bs, ts, d),
            jnp.stack(ks_, axis=1), jnp.stack(vs_, axis=1), jnp.stack(srs, axis=1), jnp.stack(sws, axis=1))
```

```python
import functools

import jax
import jax.numpy as jnp
from jax import lax
from jax.experimental import pallas as pl
from jax.experimental.pallas import tpu as pltpu

F32 = jnp.float32
BF16 = jnp.bfloat16

D_MODEL = 1024
DEPTH = 2
GRID_W = 64
MIX_W = D_MODEL // 2
N_BRANCH = 3
EPS = 1e-6
HEAD_DIM = 128
ATT_HEADS = MIX_W // HEAD_DIM
ATT_KV_HEADS = ATT_HEADS // 2
ROPE_THETA = 10000.0
RET_HEADS = MIX_W // HEAD_DIM
RET_CHUNK = 128
RWKV_N = 64
RWKV_HEADS = MIX_W // RWKV_N
RWKV_CHUNK = 64
RWKV_GROUP = 4
RWKV_LORA = 256
RWKV_LN_EPS = 64e-5
N_EXPERTS = 32
TOP_K = 4
D_FF = D_MODEL
SWIGLU_LIMIT = 7.0
SWIGLU_ALPHA = 1.702
RET_COLS = 4 * MIX_W
RWKV_COLS = 3 * MIX_W + RWKV_LORA
ATT_COLS = (ATT_HEADS + 2 * ATT_KV_HEADS) * HEAD_DIM
GATE_COLS = N_BRANCH * D_MODEL
MAIN_COLS = RET_COLS + RWKV_COLS + ATT_COLS
N_MOD = 6
MOD_ROWS = 8

LANE = 128
SUBLANE = 8
VMEM_LIMIT = 56 * 1024 * 1024

_NT = (((1,), (1,)), ((), ()))
_TN = (((0,), (0,)), ((), ()))


def _cparams(*sem):
    return pltpu.CompilerParams(dimension_semantics=sem, vmem_limit_bytes=VMEM_LIMIT)


def _bdot(a, b):
    return jnp.dot(a.astype(BF16), b.astype(BF16), preferred_element_type=F32)


def _bdot_g(a, b, dims):
    return lax.dot_general(a.astype(BF16), b.astype(BF16), dims, preferred_element_type=F32)


def _split3(x):
    hi = x.astype(BF16)
    r1 = x - hi.astype(F32)
    mid = r1.astype(BF16)
    lo = (r1 - mid.astype(F32)).astype(BF16)
    return hi, mid, lo


def _dot_x3(a, b):
    ah = a.astype(BF16)
    bh = b.astype(BF16)
    al = (a - ah.astype(F32)).astype(BF16)
    bl = (b - bh.astype(F32)).astype(BF16)
    d = functools.partial(jnp.dot, preferred_element_type=F32)
    return d(ah, bh) + (d(ah, bl) + d(al, bh))


def _dot_exact_rhs(a, b_exact):
    hi, mid, lo = _split3(a)
    d = functools.partial(jnp.dot, preferred_element_type=F32)
    return d(hi, b_exact) + (d(mid, b_exact) + d(lo, b_exact))


def _sigmoid(x):
    return 1.0 / (1.0 + jnp.exp(-x))


def _silu(x):
    return x * _sigmoid(x)


def _mod_kernel(cond_ref, w_ref, b_ref, o_ref):
    s = _silu(cond_ref[...])
    o_ref[...] = _dot_x3(s, w_ref[...]) + b_ref[...]


def modulation(cond_rows, mod_w, mod_b):
    n_layers, d, cols = mod_w.shape
    tn = d
    return pl.pallas_call(
        _mod_kernel,
        out_shape=jax.ShapeDtypeStruct((n_layers, MOD_ROWS, cols), F32),
        grid=(n_layers, cols // tn),
        in_specs=[
            pl.BlockSpec((MOD_ROWS, d), lambda l, j: (0, 0)),
            pl.BlockSpec((None, d, tn), lambda l, j: (l, 0, j)),
            pl.BlockSpec((None, 1, tn), lambda l, j: (l, 0, j)),
        ],
        out_specs=pl.BlockSpec((None, MOD_ROWS, tn), lambda l, j: (l, 0, j)),
        compiler_params=_cparams("arbitrary", "arbitrary"),
        name="modulation",
    )(cond_rows, mod_w, mod_b.reshape(n_layers, 1, cols))


def _mod_row(i, mod_base, tiles_per_seq):
    if mod_base == 0:
        return 0
    return mod_base + i // tiles_per_seq


def _modnorm(x, nw, shift, scale):
    ms = jnp.mean(x * x, axis=-1, keepdims=True)
    return (x * lax.rsqrt(ms + EPS)) * nw * (1.0 + scale) + shift


def _in_proj_kernel(x_ref, nw_ref, sh_ref, sc_ref, w_ref, o_ref, h_scr, *, mod_base, tiles_per_seq):
    i = pl.program_id(0)

    @pl.when(pl.program_id(1) == 0)
    def _():
        r = _mod_row(i, mod_base, tiles_per_seq)
        h = _modnorm(x_ref[...], nw_ref[...], sh_ref[pl.ds(r, 1), :], sc_ref[pl.ds(r, 1), :])
        h_scr[...] = h.astype(BF16)

    o_ref[...] = jnp.dot(h_scr[...], w_ref[...].astype(BF16), preferred_element_type=F32)


def in_proj(x, norm_w, mod, w_in, layer, col0, ncols, *, seq_len, mod_base):
    n, d = x.shape
    tm = min(n, 2048, seq_len if mod_base else n)
    tn = 256
    assert n % tm == 0 and ncols % tn == 0 and col0 % tn == 0 and (mod_base == 0 or seq_len % tm == 0)
    j0 = col0 // tn
    kern = functools.partial(_in_proj_kernel, mod_base=mod_base, tiles_per_seq=seq_len // tm)
    return pl.pallas_call(
        kern,
        out_shape=jax.ShapeDtypeStruct((n, ncols), F32),
        grid=(n // tm, ncols // tn),
        in_specs=[
            pl.BlockSpec((tm, d), lambda i, j: (i, 0)),
            pl.BlockSpec((None, 1, d), lambda i, j: (layer, 0, 0)),
            pl.BlockSpec((None, MOD_ROWS, d), lambda i, j: (layer, 0, 0)),
            pl.BlockSpec((None, MOD_ROWS, d), lambda i, j: (layer, 0, 1)),
            pl.BlockSpec((None, d, tn), lambda i, j: (layer, 0, j0 + j)),
        ],
        out_specs=pl.BlockSpec((tm, tn), lambda i, j: (i, j)),
        scratch_shapes=[pltpu.VMEM((tm, d), BF16)],
        compiler_params=_cparams("arbitrary", "arbitrary"),
        name="in_proj",
    )(x, norm_w, mod, mod, w_in)


def _rope(x, cos, sin_even, sin_odd):
    nxt = pltpu.roll(x, LANE - 1, axis=1)
    prv = pltpu.roll(x, 1, axis=1)
    return x * cos + nxt * sin_even + prv * sin_odd


def _rms(x, w):
    return x * lax.rsqrt(jnp.mean(x * x, axis=-1, keepdims=True) + EPS) * w


def _iota2(shape, dim):
    return lax.broadcasted_iota(jnp.int32, shape, dim)


def _retention_kernel(*refs, layer, latent, n_chunks):
    if latent:
        (decay_ref, q_ref, k_ref, v_ref, g_ref, cos_ref, se_ref, so_ref, s0_ref,
         o_ref, q_scr, k_scr, kf_scr, kb_scr) = refs
    else:
        (decay_ref, q_ref, k_ref, v_ref, g_ref,
         o_ref, sout_ref, q_scr, k_scr, kf_scr, kb_scr) = refs
    c_len = RET_CHUNK
    h = pl.program_id(1)
    lgf = -decay_ref[layer, 0, h]
    lgb = -decay_ref[layer, 1, h]

    if latent:
        q_scr[...] = _rope(q_ref[...], cos_ref[...], se_ref[...], so_ref[...])
        k_scr[...] = _rope(k_ref[...], cos_ref[...], se_ref[...], so_ref[...]) * (HEAD_DIM ** -0.5)
    else:
        q_scr[...] = q_ref[...]
        k_scr[...] = k_ref[...] * (HEAD_DIM ** -0.5)

    ii = _iota2((c_len, c_len), 0)
    jj = _iota2((c_len, c_len), 1)
    diff = (ii - jj).astype(F32)
    tok = ii.astype(F32)
    dcomb = (jnp.where(diff >= 0, jnp.exp(lgf * jnp.maximum(diff, 0.0)), 0.0)
             + jnp.where(diff <= 0, jnp.exp(lgb * jnp.maximum(-diff, 0.0)), 0.0))
    xi_f = jnp.exp(lgf * (tok + 1.0))
    xi_b = jnp.exp(lgb * (c_len - tok))
    zeta_f = jnp.exp(lgf * (c_len - 1.0 - tok))
    zeta_b = jnp.exp(lgb * tok)
    gc_f = jnp.exp(jnp.full((c_len, c_len), c_len, F32) * lgf)
    gc_b = jnp.exp(jnp.full((c_len, c_len), c_len, F32) * lgb)

    def rows(c):
        return pl.ds(pl.multiple_of(c * c_len, c_len), c_len)

    def local(c, carry):
        kc = k_scr[rows(c), :]
        vc = v_ref[rows(c), :]
        kf_scr[c] = _bdot_g(kc * zeta_f, vc, _TN)
        kb_scr[c] = _bdot_g(kc * zeta_b, vc, _TN)
        return carry

    lax.fori_loop(0, n_chunks, local, 0)

    if latent:
        sf0 = s0_ref[0]
        sb0 = s0_ref[1]
    else:
        sf0 = jnp.zeros((c_len, c_len), F32)
        sb0 = sf0

    def scan_f(c, s):
        kl = kf_scr[c]
        kf_scr[c] = s
        return s * gc_f + kl

    def scan_b(t, s):
        c = n_chunks - 1 - t
        kl = kb_scr[c]
        kb_scr[c] = s
        return s * gc_b + kl

    sf = lax.fori_loop(0, n_chunks, scan_f, sf0)
    sb = lax.fori_loop(0, n_chunks, scan_b, sb0)
    if not latent:
        sout_ref[0] = sf
        sout_ref[1] = sb

    def output(c, carry):
        qc = q_scr[rows(c), :]
        kc = k_scr[rows(c), :]
        vc = v_ref[rows(c), :]
        att = _bdot_g(qc, kc, _NT) * dcomb
        qx = jnp.concatenate([qc * xi_f, qc * xi_b], axis=1)
        ss = jnp.concatenate([kf_scr[c], kb_scr[c]], axis=0)
        o = _bdot(att, vc) + _bdot(qx, ss)
        mu = jnp.mean(o, axis=-1, keepdims=True)
        d = o - mu
        var = jnp.mean(d * d, axis=-1, keepdims=True)
        y = d * lax.rsqrt(var + EPS)
        o_ref[rows(c), :] = _silu(g_ref[rows(c), :]) * y
        return carry

    lax.fori_loop(0, n_chunks, output, 0)


def retention(z, ret_decay, layer, batch, seq_len, *, rope=None, state0=None):
    latent = rope is not None
    n = z.shape[0]
    t = seq_len
    n_chunks = t // RET_CHUNK
    hd = HEAD_DIM
    nh = RET_HEADS
    col = lambda base: pl.BlockSpec((t, hd), lambda b, h, base=base: (b, base + h))
    in_specs = [pl.BlockSpec(memory_space=pltpu.SMEM), col(0), col(nh), col(2 * nh), col(3 * nh)]
    args = [ret_decay, z, z, z, z]
    out_shape = [jax.ShapeDtypeStruct((n, MIX_W), F32)]
    out_specs = [pl.BlockSpec((t, hd), lambda b, h: (b, h))]
    if latent:
        tab = pl.BlockSpec((t, hd), lambda b, h: (0, 0))
        in_specs += [tab, tab, tab,
                     pl.BlockSpec((None, None, 2, None, hd, hd), lambda b, h: (b, layer, 0, h, 0, 0))]
        args += [*rope, state0]
    else:
        out_shape.append(jax.ShapeDtypeStruct((batch, 2, nh, hd, hd), F32))
        out_specs.append(pl.BlockSpec((None, 2, None, hd, hd), lambda b, h: (b, 0, h, 0, 0)))
    kern = functools.partial(_retention_kernel, layer=layer, latent=latent, n_chunks=n_chunks)
    res = pl.pallas_call(
        kern,
        out_shape=out_shape,
        grid=(batch, nh),
        in_specs=in_specs,
        out_specs=out_specs,
        scratch_shapes=[pltpu.VMEM((t, hd), F32), pltpu.VMEM((t, hd), F32),
                        pltpu.VMEM((n_chunks, hd, hd), F32), pltpu.VMEM((n_chunks, hd, hd), F32)],
        compiler_params=_cparams("arbitrary", "arbitrary"),
        name="retention",
    )(*args)
    return res if not latent else res[0]


def _attention_kernel(*refs, latent, tq, past):
    if latent:
        (q_ref, k_ref, v_ref, qn_ref, kn_ref, cos_ref, se_ref, so_ref, ck_ref, cv_ref,
         o_ref, kb_scr, vb_scr) = refs
    else:
        (q_ref, k_ref, v_ref, qn_ref, kn_ref,
         o_ref, nk_ref, nv_ref, kb_scr, vb_scr) = refs
    qi = pl.program_id(2)
    hd = HEAD_DIM

    @pl.when(qi == 0)
    def _():
        kn = _rms(k_ref[...], kn_ref[...])
        if latent:
            kb_scr[0:past, :] = ck_ref[...].astype(BF16)
            vb_scr[0:past, :] = cv_ref[...].astype(BF16)
            kb_scr[past:, :] = _rope(kn, cos_ref[...], se_ref[...], so_ref[...]).astype(BF16)
            vb_scr[past:, :] = v_ref[...].astype(BF16)
        else:
            kb_scr[...] = kn.astype(BF16)
            vb_scr[...] = v_ref[...].astype(BF16)
            nk_ref[...] = kn
            nv_ref[...] = v_ref[...]

    q = q_ref[...]
    heads = []
    for g in range(2):
        qg = _rms(q[:, g * hd:(g + 1) * hd], qn_ref[...])
        if latent:
            r = pl.ds(pl.multiple_of(qi * tq, tq), tq)
            qg = _rope(qg, cos_ref[r, :], se_ref[r, :], so_ref[r, :])
        heads.append(qg)
    qq = jnp.concatenate(heads, axis=0)
    s = _bdot_g(qq, kb_scr[...], _NT) * (hd ** -0.5)
    m = jnp.max(s, axis=-1, keepdims=True)
    p = jnp.exp(s - m)
    l = jnp.sum(p, axis=-1, keepdims=True)
    o = jnp.dot(p.astype(BF16), vb_scr[...], preferred_element_type=F32) / l
    o_ref[...] = jnp.concatenate([o[:tq], o[tq:]], axis=1)


def attention(z, q_norm_w, k_norm_w, layer, batch, seq_len, *, rope=None, cache_k=None, cache_v=None):
    latent = rope is not None
    n = z.shape[0]
    t = seq_len
    hd = HEAD_DIM
    tq = min(t, 256)
    past = cache_k.shape[2] if latent else 0
    q0 = (RET_COLS + RWKV_COLS) // (2 * hd)
    k0 = (RET_COLS + RWKV_COLS) // hd + ATT_HEADS
    v0 = k0 + ATT_KV_HEADS
    nq = t // tq
    in_specs = [
        pl.BlockSpec((tq, 2 * hd), lambda b, g, i: (b * nq + i, q0 + g)),
        pl.BlockSpec((t, hd), lambda b, g, i: (b, k0 + g)),
        pl.BlockSpec((t, hd), lambda b, g, i: (b, v0 + g)),
        pl.BlockSpec((None, 1, hd), lambda b, g, i: (layer, 0, 0)),
        pl.BlockSpec((None, 1, hd), lambda b, g, i: (layer, 0, 0)),
    ]
    args = [z, z, z, q_norm_w, k_norm_w]
    out_shape = [jax.ShapeDtypeStruct((n, MIX_W), F32)]
    out_specs = [pl.BlockSpec((tq, 2 * hd), lambda b, g, i: (b * nq + i, g))]
    if latent:
        tab = pl.BlockSpec((t, hd), lambda b, g, i: (0, 0))
        cache = pl.BlockSpec((None, None, past, hd), lambda b, g, i: (b, layer, 0, g))
        in_specs += [tab, tab, tab, cache, cache]
        flat = cache_k.shape[:3] + (ATT_KV_HEADS * hd,)
        args += [*rope, cache_k.reshape(flat), cache_v.reshape(flat)]
    else:
        new = jax.ShapeDtypeStruct((batch, t, ATT_KV_HEADS * hd), F32)
        out_shape += [new, new]
        spec = pl.BlockSpec((None, t, hd), lambda b, g, i: (b, 0, g))
        out_specs += [spec, spec]
    kern = functools.partial(_attention_kernel, latent=latent, tq=tq, past=past)
    res = pl.pallas_call(
        kern,
        out_shape=out_shape,
        grid=(batch, ATT_KV_HEADS, nq),
        in_specs=in_specs,
        out_specs=out_specs,
        scratch_shapes=[pltpu.VMEM((past + t, hd), BF16), pltpu.VMEM((past + t, hd), BF16)],
        compiler_params=_cparams("arbitrary", "arbitrary", "arbitrary"),
        name="attention",
    )(*args)
    return res if not latent else res[0]


def _softplus(x):
    return jnp.maximum(x, 0.0) + jnp.log1p(jnp.exp(-jnp.abs(x)))


def _token_shift(u, prev_row, next_row, mu):
    tr = u.shape[0]
    row = _iota2(u.shape, 0)
    prev = jnp.where(row == 0, prev_row, pltpu.roll(u, 1, axis=0))
    nxt = jnp.where(row == tr - 1, next_row, pltpu.roll(u, tr - 1, axis=0))
    return u + mu[0:1, :] * (prev - u) + mu[1:2, :] * (nxt - u)


def _rwkv_prep_kernel(zr, zk, zv, zl, pr, pk, pv, pl_, nr, nk, nv, nl, mu_r, mu_k, mu_v, mu_l,
                      w0_ref, wup_ref, a0_ref, aup_ref, gup_ref, kkw_ref, kaw_ref, rkw_ref, seg_ref,
                      r_o, k_o, v_o, kk_o, b_o, lw0_o, lw1_o, bonus_o, gate_o, *, tiles_per_seq):
    i = pl.program_id(0)
    pos = i % tiles_per_seq
    keep_prev = jnp.where(pos == 0, 0.0, 1.0)
    keep_next = jnp.where(pos == tiles_per_seq - 1, 0.0, 1.0)

    def shifted(cur, prv, nxt, mu):
        return _token_shift(cur[...], prv[SUBLANE - 1:SUBLANE, :] * keep_prev, nxt[0:1, :] * keep_next, mu[...])

    wr = shifted(zr, pr, nr, mu_r)
    wk = shifted(zk, pk, nk, mu_k)
    wv = shifted(zv, pv, nv, mu_v)
    lo = shifted(zl, pl_, nl, mu_l)

    seg = seg_ref[...]
    icl = _sigmoid(a0_ref[...] + _dot_x3(lo, aup_ref[...]))
    gate_o[...] = _dot_x3(_sigmoid(lo), gup_ref[...])
    th = jnp.tanh(lo)
    for d, lw_o in enumerate((lw0_o, lw1_o)):
        wl = w0_ref[d:d + 1, :] + _dot_x3(th, wup_ref[d])
        lw_o[...] = -jnp.exp(-_softplus(-wl) - 0.5)
    kk = wk * kkw_ref[...]
    kk = kk * lax.rsqrt(_dot_exact_rhs(kk * kk, seg) + 1e-12)
    k = wk * (1.0 + (icl - 1.0) * kaw_ref[...])
    r_o[...] = wr
    k_o[...] = k
    v_o[...] = wv
    kk_o[...] = kk
    b_o[...] = kk * icl
    bonus_o[...] = _dot_exact_rhs(wr * k * rkw_ref[...], seg) * wv


def rwkv_prep(z, p, layer, seq_len):
    n = z.shape[0]
    tr = 256
    w = MIX_W
    c0 = RET_COLS // w
    cl = (RET_COLS + 3 * w) // RWKV_LORA
    nt = n // tr
    rb = tr // SUBLANE
    last8 = n // SUBLANE - 1
    cur = lambda width, c: pl.BlockSpec((tr, width), lambda i, c=c: (i, c))
    prv = lambda width, c: pl.BlockSpec((SUBLANE, width), lambda i, c=c: (jnp.maximum(i * rb - 1, 0), c))
    nxt = lambda width, c: pl.BlockSpec((SUBLANE, width), lambda i, c=c: (jnp.minimum((i + 1) * rb, last8), c))
    cols = [(w, c0), (w, c0 + 1), (w, c0 + 2), (RWKV_LORA, cl)]
    mu_cols = [(w, 0), (w, 1), (w, 2), (RWKV_LORA, 3 * w // RWKV_LORA)]
    whole = lambda *shape: pl.BlockSpec((None,) + shape, lambda i: (layer,) + (0,) * len(shape))
    in_specs = ([cur(*c) for c in cols] + [prv(*c) for c in cols] + [nxt(*c) for c in cols]
                + [pl.BlockSpec((None, 2, width), lambda i, c=c: (layer, 0, c)) for width, c in mu_cols]
                + [whole(2, w), whole(2, RWKV_LORA, w), whole(1, w), whole(RWKV_LORA, w), whole(RWKV_LORA, w),
                   whole(1, w), whole(1, w), whole(1, w), pl.BlockSpec((w, w), lambda i: (0, 0))])
    out = jax.ShapeDtypeStruct((n, w), F32)
    kern = functools.partial(_rwkv_prep_kernel, tiles_per_seq=seq_len // tr)
    return pl.pallas_call(
        kern,
        out_shape=[out] * 9,
        grid=(nt,),
        in_specs=in_specs,
        out_specs=[pl.BlockSpec((tr, w), lambda i: (i, 0))] * 9,
        compiler_params=_cparams("arbitrary"),
        name="rwkv_prep",
    )(*([z] * 12), *([p["rwkv_mu"]] * 4), p["rwkv_w0"], p["w_up_pad"], p["rwkv_a0"], p["a_up_pad"], p["g_up_pad"],
      p["rwkv_k_k"], p["rwkv_k_a"], p["rwkv_r_k"], p["seg_ones"])


def _tri_inverse_steps(mats, eye, blk16, off32, off64):
    ds = [jnp.where(blk16, a, 0.0) for a in mats]
    ts = [eye + d for d in ds]
    pws = ds
    for _ in range(3):
        pws = [_bdot(pw, pw) for pw in pws]
        yield None
        ts = [_bdot(t, eye + pw) for t, pw in zip(ts, pws)]
        yield None
    for off in (off32, off64):
        halves = [_bdot(t, jnp.where(off, a, 0.0)) for t, a in zip(ts, mats)]
        yield None
        ts = [t + _bdot(h, t) for t, h in zip(ts, halves)]
        yield None
    yield ts


def _rwkv_scan_kernel(*refs, latent, n_chunks, group):
    scratch = refs[-10:]
    yf_scr, yb_scr, s_scr, ar_s, tinv_s, av_s, mrbk_s, vst_s, bko_s, gt_s = scratch
    if latent:
        (r_ref, k_ref, v_ref, kk_ref, b_ref, lw0_ref, lw1_ref, bonus_ref, gate_ref, lnw_ref, s0_ref,
         o_ref) = refs[:-10]
    else:
        (r_ref, k_ref, v_ref, kk_ref, b_ref, lw0_ref, lw1_ref, bonus_ref, gate_ref, lnw_ref,
         o_ref, sout_ref) = refs[:-10]
    cl = RWKV_CHUNK
    hn = RWKV_N
    two = 2 * cl
    m0 = _iota2((1, LANE), 1) < hn
    row = _iota2((two, two), 0)
    col = _iota2((two, two), 1)
    same = (row >= cl) == (col >= cl)
    eye = jnp.where(row == col, 1.0, 0.0)
    blk16 = (row >> 4) == (col >> 4)
    off32 = ((row >> 5) == (col >> 5)) & ~blk16
    off64 = same & ((row >> 5) != (col >> 5))
    r64 = _iota2((cl, cl), 0)
    c64 = _iota2((cl, cl), 1)
    masks = (
        (same & (row >= col), same & (row > col), jnp.where(r64 >= c64, 1.0, 0.0).astype(BF16)),
        (same & (row <= col), same & (row < col), jnp.where(r64 <= c64, 1.0, 0.0).astype(BF16)),
    )

    def stack(x):
        return jnp.concatenate([jnp.where(m0, x, 0.0), jnp.where(m0, 0.0, x)], axis=0)

    zero = jnp.zeros((hn, hn), F32)
    for d in range(2):
        if latent:
            s_scr[d] = jnp.concatenate(
                [jnp.concatenate([s0_ref[d, 0], zero], axis=1), jnp.concatenate([zero, s0_ref[d, 1]], axis=1)], axis=0)
        else:
            s_scr[d] = jnp.zeros((LANE, LANE), F32)

    lw_refs = (lw0_ref, lw1_ref)
    y_scrs = (yf_scr, yb_scr)

    def chunk_rows(g, j, d):
        ci = g * group + j
        ci = ci if d == 0 else n_chunks - 1 - ci
        return pl.ds(pl.multiple_of(ci * cl, cl), cl)

    units = [(j, d) for j in range(group) for d in range(2)]
    dot = functools.partial(jnp.dot, preferred_element_type=F32)

    def prepare_steps(g, slot):
        rows = [chunk_rows(g, j, d) for j, d in units]
        lws = [lw_refs[d][r, :] for (j, d), r in zip(units, rows)]
        parts = [_split3(lw) for lw in lws]
        lgs = [dot(masks[d][2], hi) + (dot(masks[d][2], mid) + dot(masks[d][2], lo))
               for (j, d), (hi, mid, lo) in zip(units, parts)]
        yield
        tots = [lg[cl - 1:cl, :] if d == 0 else lg[0:1, :] for (j, d), lg in zip(units, lgs)]
        ars, bks = [], []
        for i, r in enumerate(rows):
            lg, lw, tot = lgs[i], lws[i], tots[i]
            e_neg = jnp.exp(-lg)
            e_out = jnp.exp(tot - lg)
            bb = b_ref[r, :]
            kc = k_ref[r, :]
            ar = jnp.concatenate([stack(-kk_ref[r, :] * jnp.exp(lg - lw)), stack(r_ref[r, :] * jnp.exp(lg))], axis=0)
            ars.append(ar)
            bks.append(jnp.concatenate([stack(bb * e_neg), stack(kc * e_neg)], axis=0))
            ar_s[slot, i] = ar.astype(BF16)
            bko_s[slot, i] = jnp.concatenate([stack(bb * e_out), stack(kc * e_out)], axis=0).astype(BF16)
            gt_s[slot, i] = jnp.broadcast_to(jnp.exp(tot), (SUBLANE, LANE))
        yield
        crosses = [_bdot_g(ar, bk, _NT) for ar, bk in zip(ars, bks)]
        yield
        a_abs, a_aks = [], []
        for i, ((j, d), cross) in enumerate(zip(units, crosses)):
            incl, strict, _ = masks[d]
            a_abs.append(jnp.where(strict, cross[:two, :two], 0.0))
            a_aks.append(jnp.where(strict, cross[:two, two:], 0.0))
            mrbk_s[slot, i] = jnp.concatenate([jnp.where(incl, cross[two:, :two], 0.0),
                                               jnp.where(incl, cross[two:, two:], 0.0)], axis=1).astype(BF16)
        v_sts = [stack(v_ref[r, :]) for r in rows]
        for i, (a_ak, v_st) in enumerate(zip(a_aks, v_sts)):
            vst_s[slot, i] = v_st.astype(BF16)
            av_s[slot, i] = _bdot(a_ak, v_st)
        yield
        inverses = None
        for inverses in _tri_inverse_steps(a_abs, eye, blk16, off32, off64):
            yield
        for i, t_inv in enumerate(inverses):
            tinv_s[slot, i] = t_inv.astype(BF16)

    def advance_steps(g, slot):
        for j in range(group):
            dirs = [(2 * j + d, d, chunk_rows(g, j, d)) for d in range(2)]
            ss = [s_scr[d] for i, d, r in dirs]
            arss = [_bdot_g(ar_s[slot, i], s, _NT) for (i, d, r), s in zip(dirs, ss)]
            us = [dot(tinv_s[slot, i], (ars[:two] + av_s[slot, i]).astype(BF16)) for (i, d, r), ars in zip(dirs, arss)]
            uvs = [jnp.concatenate([u.astype(BF16), vst_s[slot, i]], axis=0) for (i, d, r), u in zip(dirs, us)]
            for (i, d, r), s, ars, uv in zip(dirs, ss, arss, uvs):
                y = ars[two:] + dot(mrbk_s[slot, i], uv)
                s_scr[d] = s * gt_s[slot, i][0:1, :] + lax.dot_general(uv, bko_s[slot, i], _TN,
                                                                       preferred_element_type=F32)
                y_scrs[d][r, :] = y[:cl] + y[cl:]
            yield

    def run(main, side=(), every=1):
        side = iter(side)
        for n_done, _ in enumerate(main, start=1):
            if n_done % every == 0:
                next(side, None)
        for _ in side:
            pass

    n_groups = n_chunks // group
    run(prepare_steps(0, 0))
    if n_groups > 1:
        def body(g, carry):
            run(prepare_steps(g, g % 2), advance_steps(g - 1, (g - 1) % 2), every=3)
            return carry

        lax.fori_loop(1, n_groups, body, 0)
    run(advance_steps(n_groups - 1, (n_groups - 1) % 2))

    if not latent:
        for d in range(2):
            s = s_scr[d]
            sout_ref[d, 0] = s[:hn, :hn]
            sout_ref[d, 1] = s[hn:, hn:]

    rt = 256 if (n_chunks * cl) % 256 == 0 else n_chunks * cl

    def epilogue(j, carry):
        rows = pl.ds(pl.multiple_of(j * rt, rt), rt)
        y = yf_scr[rows, :] + yb_scr[rows, :]
        s_all = jnp.sum(y, axis=-1, keepdims=True)
        s_0 = jnp.sum(jnp.where(m0, y, 0.0), axis=-1, keepdims=True)
        mu = jnp.where(m0, s_0, s_all - s_0) * (1.0 / hn)
        dlt = y - mu
        q_all = jnp.sum(dlt * dlt, axis=-1, keepdims=True)
        q_0 = jnp.sum(jnp.where(m0, dlt * dlt, 0.0), axis=-1, keepdims=True)
        var = jnp.where(m0, q_0, q_all - q_0) * (1.0 / hn)
        yn = dlt * lax.rsqrt(var + RWKV_LN_EPS) * lnw_ref[...] + bonus_ref[rows, :]
        o_ref[rows, :] = yn * gate_ref[rows, :]
        return carry

    lax.fori_loop(0, (n_chunks * cl) // rt, epilogue, 0)


def rwkv_scan(ops, ln_w, layer, batch, seq_len, *, state0=None):
    latent = state0 is not None
    n = ops[0].shape[0]
    t = seq_len
    hn = RWKV_N
    npair = RWKV_HEADS // 2
    blk = pl.BlockSpec((t, LANE), lambda b, h: (b, h))
    in_specs = [blk] * 9 + [pl.BlockSpec((None, 1, LANE), lambda b, h: (layer, 0, h))]
    args = [*ops, ln_w]
    out_shape = [jax.ShapeDtypeStruct((n, MIX_W), F32)]
    out_specs = [blk]
    if latent:
        in_specs.append(pl.BlockSpec((None, None, 2, 2, hn, hn), lambda b, h: (b, layer, 0, h, 0, 0)))
        args.append(state0)
    else:
        out_shape.append(jax.ShapeDtypeStruct((batch, 2, RWKV_HEADS, hn, hn), F32))
        out_specs.append(pl.BlockSpec((None, 2, 2, hn, hn), lambda b, h: (b, 0, h, 0, 0)))
    n_chunks = t // RWKV_CHUNK
    group = min(RWKV_GROUP, n_chunks)
    assert n_chunks % group == 0
    units = 2 * group
    two = 2 * RWKV_CHUNK
    kern = functools.partial(_rwkv_scan_kernel, latent=latent, n_chunks=n_chunks, group=group)
    res = pl.pallas_call(
        kern,
        out_shape=out_shape,
        grid=(batch, npair),
        in_specs=in_specs,
        out_specs=out_specs,
        scratch_shapes=[
            pltpu.VMEM((t, LANE), F32), pltpu.VMEM((t, LANE), F32), pltpu.VMEM((2, LANE, LANE), F32),
            pltpu.VMEM((2, units, 2 * two, LANE), BF16),
            pltpu.VMEM((2, units, two, two), BF16),
            pltpu.VMEM((2, units, two, LANE), F32),
            pltpu.VMEM((2, units, two, 2 * two), BF16),
            pltpu.VMEM((2, units, two, LANE), BF16),
            pltpu.VMEM((2, units, 2 * two, LANE), BF16),
            pltpu.VMEM((2, units, SUBLANE, LANE), F32),
        ],
        compiler_params=_cparams("arbitrary", "arbitrary"),
        name="rwkv_scan",
    )(*args)
    return res if not latent else res[0]


def _topk_softmax(logits):
    lane = _iota2(logits.shape, 1)
    out_lane = _iota2((logits.shape[0], TOP_K), 1)
    n_exp = logits.shape[1]
    vals, idxs = [], []
    work = logits
    for _ in range(TOP_K):
        m = jnp.max(work, axis=-1, keepdims=True)
        idx = jnp.min(jnp.where(work == m, lane, n_exp), axis=-1, keepdims=True)
        vals.append(m)
        idxs.append(idx)
        work = jnp.where(lane == idx, -jnp.inf, work)
    es = [jnp.exp(v - vals[0]) for v in vals]
    denom = es[0] + es[1] + es[2] + es[3]
    top_i = jnp.zeros((logits.shape[0], TOP_K), jnp.int32)
    top_p = jnp.zeros((logits.shape[0], TOP_K), F32)
    for j in range(TOP_K):
        top_i = jnp.where(out_lane == j, idxs[j], top_i)
        top_p = jnp.where(out_lane == j, es[j] / denom, top_p)
    return top_i, top_p


def _merge_kernel(x_ref, oret_ref, orwkv_ref, oatt_ref, zg_ref, wb_ref, wo_ref, g1_ref, nw2_ref, sh2_ref, sc2_ref,
                  rw_ref, rb_ref, x1_ref, h2_ref, ti_ref, tp_ref, *, mod_base, tiles_per_seq):
    r = _mod_row(pl.program_id(0), mod_base, tiles_per_seq)
    d = x_ref.shape[1]
    acc = None
    for nb, o_ref in enumerate((oret_ref, orwkv_ref, oatt_ref)):
        proj = jnp.dot(o_ref[...].astype(BF16), wb_ref[nb], preferred_element_type=F32)
        term = _sigmoid(zg_ref[:, nb * d:(nb + 1) * d]) * proj
        acc = term if acc is None else acc + term
    out = jnp.dot(acc.astype(BF16), wo_ref[...], preferred_element_type=F32)
    x1 = x_ref[...] + g1_ref[pl.ds(r, 1), :] * out
    x1_ref[...] = x1
    h2 = _modnorm(x1, nw2_ref[...], sh2_ref[pl.ds(r, 1), :], sc2_ref[pl.ds(r, 1), :])
    h2_ref[...] = h2
    logits = _dot_x3(h2, rw_ref[...]) + rb_ref[...]
    top_i, top_p = _topk_softmax(logits)
    ti_ref[...] = top_i
    tp_ref[...] = top_p


def merge(x, o_ret, o_rwkv, o_att, zg, p, mod, layer, *, seq_len, mod_base):
    n, d = x.shape
    tm = 256
    w = MIX_W
    row = lambda width: pl.BlockSpec((tm, width), lambda i: (i, 0))
    modc = lambda c: pl.BlockSpec((None, MOD_ROWS, d), lambda i, c=c: (layer, 0, c))
    kern = functools.partial(_merge_kernel, mod_base=mod_base, tiles_per_seq=seq_len // tm)
    return pl.pallas_call(
        kern,
        out_shape=[jax.ShapeDtypeStruct((n, d), F32), jax.ShapeDtypeStruct((n, d), F32),
                   jax.ShapeDtypeStruct((n, TOP_K), jnp.int32), jax.ShapeDtypeStruct((n, TOP_K), F32)],
        grid=(n // tm,),
        in_specs=[
            row(d), row(w), row(w), row(w), row(N_BRANCH * d),
            pl.BlockSpec((None, N_BRANCH, w, d), lambda i: (layer, 0, 0, 0)),
            pl.BlockSpec((None, d, d), lambda i: (layer, 0, 0)),
            modc(2),
            pl.BlockSpec((None, 1, d), lambda i: (layer, 0, 0)),
            modc(3), modc(4),
            pl.BlockSpec((None, d, N_EXPERTS), lambda i: (layer, 0, 0)),
            pl.BlockSpec((None, 1, N_EXPERTS), lambda i: (layer, 0, 0)),
        ],
        out_specs=[row(d), row(d), row(TOP_K), row(TOP_K)],
        compiler_params=_cparams("arbitrary"),
        name="merge",
    )(x, o_ret, o_rwkv, o_att, zg, p["w_branch_bf"], p["w_out_bf"], mod, p["norm2_w"], mod, mod,
      p["router_w"], p["router_b"])


def _moe_kernel(te_ref, nu_ref, xs_ref, wgu_ref, bgu_ref, wdn_ref, bdn_ref, y_ref, wgu_scr, wdn_scr):
    i = pl.program_id(0)
    prev = te_ref[jnp.maximum(i - 1, 0)]

    @pl.when((i == 0) | (te_ref[i] != prev))
    def _():
        wgu_scr[...] = wgu_ref[...].astype(BF16)
        wdn_scr[...] = wdn_ref[...].astype(BF16)

    @pl.when(i >= nu_ref[0])
    def _():
        y_ref[...] = jnp.zeros_like(y_ref)

    @pl.when(i < nu_ref[0])
    def _():
        f = wdn_scr.shape[0]
        gu = jnp.dot(xs_ref[...].astype(BF16), wgu_scr[...], preferred_element_type=F32) + bgu_ref[...]
        gate = jnp.minimum(gu[:, :f], SWIGLU_LIMIT)
        up = jnp.clip(gu[:, f:], -SWIGLU_LIMIT, SWIGLU_LIMIT)
        act = (up + 1.0) * gate * _sigmoid(SWIGLU_ALPHA * gate)
        y_ref[...] = jnp.dot(act.astype(BF16), wdn_scr[...], preferred_element_type=F32) + bdn_ref[...]


def moe_experts(xs, tile_expert, n_used, p, layer, tile):
    s, d = xs.shape
    f = D_FF
    grid_spec = pltpu.PrefetchScalarGridSpec(
        num_scalar_prefetch=2,
        grid=(s // tile,),
        in_specs=[
            pl.BlockSpec((tile, d), lambda i, te, nu: (jnp.minimum(i, nu[0] - 1), 0)),
            pl.BlockSpec((None, None, d, 2 * f), lambda i, te, nu: (layer, te[i], 0, 0)),
            pl.BlockSpec((None, None, 1, 2 * f), lambda i, te, nu: (layer, te[i], 0, 0)),
            pl.BlockSpec((None, None, f, d), lambda i, te, nu: (layer, te[i], 0, 0)),
            pl.BlockSpec((None, None, 1, d), lambda i, te, nu: (layer, te[i], 0, 0)),
        ],
        out_specs=pl.BlockSpec((tile, d), lambda i, te, nu: (i, 0)),
        scratch_shapes=[pltpu.VMEM((d, 2 * f), BF16), pltpu.VMEM((f, d), BF16)],
    )
    return pl.pallas_call(
        _moe_kernel,
        out_shape=jax.ShapeDtypeStruct((s, d), F32),
        grid_spec=grid_spec,
        compiler_params=_cparams("arbitrary"),
        name="moe_experts",
    )(tile_expert, n_used, xs, p["w_gate_up"], p["b_gate_up"], p["w_down"], p["b_down"])


def _choice_onehots(top_i):
    lane = _iota2((top_i.shape[0], N_EXPERTS), 1)
    return [jnp.where(top_i[:, k:k + 1] == lane, 1.0, 0.0) for k in range(TOP_K)]


def _per_choice(onehots, table):
    out_lane = _iota2((onehots[0].shape[0], TOP_K), 1)
    out = jnp.zeros((onehots[0].shape[0], TOP_K), F32)
    for k, oh in enumerate(onehots):
        out = jnp.where(out_lane == k, jnp.sum(oh * table, axis=-1, keepdims=True), out)
    return out


def _route_rank_kernel(ti_ref, rank_ref, cnt_ref, run_scr):
    @pl.when(pl.program_id(0) == 0)
    def _():
        run_scr[...] = jnp.zeros_like(run_scr)

    tm = ti_ref.shape[0]
    ohs = _choice_onehots(ti_ref[...])
    chosen = ohs[0] + ohs[1] + ohs[2] + ohs[3]
    earlier = jnp.where(_iota2((tm, tm), 0) > _iota2((tm, tm), 1), 1.0, 0.0).astype(BF16)
    before = jnp.dot(earlier, chosen.astype(BF16), preferred_element_type=F32) + run_scr[...]
    rank_ref[...] = _per_choice(ohs, before).astype(jnp.int32)
    run_scr[...] += jnp.sum(chosen, axis=0, keepdims=True)
    cnt_ref[...] = run_scr[...].astype(jnp.int32)


def _route_pos_kernel(ti_ref, rank_ref, off_ref, pos_ref):
    ohs = _choice_onehots(ti_ref[...])
    pos_ref[...] = rank_ref[...] + _per_choice(ohs, off_ref[...]).astype(jnp.int32)


def route(top_i, tile):
    n = top_i.shape[0]
    tm = 256
    tok = pl.BlockSpec((tm, TOP_K), lambda i: (i, 0))
    per_e = pl.BlockSpec((1, N_EXPERTS), lambda i: (0, 0))
    rank, counts = pl.pallas_call(
        _route_rank_kernel,
        out_shape=[jax.ShapeDtypeStruct((n, TOP_K), jnp.int32), jax.ShapeDtypeStruct((1, N_EXPERTS), jnp.int32)],
        grid=(n // tm,),
        in_specs=[tok],
        out_specs=[tok, per_e],
        scratch_shapes=[pltpu.VMEM((1, N_EXPERTS), F32)],
        compiler_params=_cparams("arbitrary"),
        name="route_rank",
    )(top_i)
    counts = counts[0]
    padded = ((counts + tile - 1) // tile) * tile
    ends = jnp.cumsum(padded)
    n_tiles = (n * TOP_K) // tile + N_EXPERTS
    dump = n_tiles * tile
    tile_start = jnp.arange(n_tiles + 1, dtype=jnp.int32) * tile
    tile_expert = jnp.minimum(jnp.sum((tile_start[:, None] >= ends[None, :]).astype(jnp.int32), axis=1), N_EXPERTS - 1)
    n_used = (ends[-1:] // tile).astype(jnp.int32)
    last_tile = jnp.where(padded > 0, ends - tile, dump).astype(jnp.int32)
    pos = pl.pallas_call(
        _route_pos_kernel,
        out_shape=jax.ShapeDtypeStruct((n, TOP_K), jnp.int32),
        grid=(n // tm,),
        in_specs=[tok, tok, per_e],
        out_specs=tok,
        compiler_params=_cparams("arbitrary"),
        name="route_pos",
    )(top_i, rank, (ends - padded).astype(F32).reshape(1, N_EXPERTS))
    return pos, tile_expert.astype(jnp.int32), n_used, last_tile


_LOG2_TOP_K = TOP_K.bit_length() - 1
assert 1 << _LOG2_TOP_K == TOP_K


def _row_copy(src_ref, src_row, dst_ref, dst_row, sem):
    return pltpu.make_async_copy(src_ref.at[pl.ds(src_row, 1), :], dst_ref.at[pl.ds(dst_row, 1), :], sem)


def _dispatch_kernel(pos_ref, last_ref, nu_ref, h_ref, xs_ref, zero_scr, sem, *, tile):
    n_pairs = pos_ref.shape[0]

    @pl.when(pl.program_id(0) == 0)
    def _():
        zero_scr[...] = jnp.zeros_like(zero_scr)

        def fill(row):
            return pltpu.make_async_copy(zero_scr, xs_ref.at[pl.ds(pl.multiple_of(row, tile), tile), :], sem)

        fills = [fill(last_ref[e]) for e in range(N_EXPERTS)]
        for c in fills:
            c.start()
        for c in fills:
            c.wait()

        def fill_unused(j, carry):
            c = fill(j * tile)
            c.start()
            c.wait()
            return carry

        lax.fori_loop(nu_ref[0], xs_ref.shape[0] // tile, fill_unused, 0)

    def issue(r, carry):
        _row_copy(h_ref, lax.shift_right_logical(r, _LOG2_TOP_K), xs_ref, pos_ref[r], sem).start()
        return carry

    def drain(r, carry):
        _row_copy(h_ref, 0, xs_ref, 0, sem).wait()
        return carry

    lax.fori_loop(0, n_pairs, issue, 0, unroll=8)
    lax.fori_loop(0, n_pairs, drain, 0, unroll=8)


def dispatch(h, pos_flat, last_tile, n_used, tile):
    n, d = h.shape
    tm = 256
    n_slots = n * TOP_K + (N_EXPERTS + 1) * tile
    return pl.pallas_call(
        functools.partial(_dispatch_kernel, tile=tile),
        out_shape=jax.ShapeDtypeStruct((n_slots, d), F32),
        grid=(n // tm,),
        in_specs=[
            pl.BlockSpec((tm * TOP_K,), lambda i: (i,), memory_space=pltpu.SMEM),
            pl.BlockSpec(memory_space=pltpu.SMEM),
            pl.BlockSpec(memory_space=pltpu.SMEM),
            pl.BlockSpec((tm, d), lambda i: (i, 0)),
        ],
        out_specs=pl.BlockSpec(memory_space=pl.ANY),
        scratch_shapes=[pltpu.VMEM((tile, d), F32), pltpu.SemaphoreType.DMA(())],
        compiler_params=_cparams("arbitrary"),
        name="moe_dispatch",
    )(pos_flat, last_tile, n_used, h)


def _combine_kernel(pos_ref, pos_next_ref, x1_ref, tp_ref, g2_ref, fw_ref, ys_ref, o_ref, buf, sems,
                    *, n_tiles, mod_base, tiles_per_seq, final):
    i = pl.program_id(0)
    slot = i % 2
    tm = x1_ref.shape[0]
    n_pairs = tm * TOP_K

    def gather(p_ref, s):
        def issue(r, carry):
            pltpu.make_async_copy(ys_ref.at[pl.ds(p_ref[r], 1), :],
                                  buf.at[s, r & (TOP_K - 1), pl.ds(lax.shift_right_logical(r, _LOG2_TOP_K), 1), :],
                                  sems.at[s]).start()
            return carry
        lax.fori_loop(0, n_pairs, issue, 0, unroll=8)

    @pl.when(i == 0)
    def _():
        gather(pos_ref, 0)

    @pl.when(i + 1 < n_tiles)
    def _():
        gather(pos_next_ref, 1 - slot)

    def drain(r, carry):
        pltpu.make_async_copy(ys_ref.at[pl.ds(0, 1), :], buf.at[slot, 0, pl.ds(0, 1), :], sems.at[slot]).wait()
        return carry

    lax.fori_loop(0, n_pairs, drain, 0, unroll=8)

    tp = tp_ref[...]
    acc = tp[:, 0:1] * buf[slot, 0]
    for k in range(1, TOP_K):
        acc = acc + tp[:, k:k + 1] * buf[slot, k]
    r = _mod_row(i, mod_base, tiles_per_seq)
    x2 = x1_ref[...] + g2_ref[pl.ds(r, 1), :] * acc
    o_ref[...] = _rms(x2, fw_ref[...]) if final else x2


def combine(x1, top_p, pos_flat, row0, ys, mod, final_w, layer, *, seq_len, mod_base, final):
    n, d = x1.shape
    tm = 256
    assert row0 % tm == 0
    r0 = row0 // tm
    n_tiles = n // tm
    kern = functools.partial(_combine_kernel, n_tiles=n_tiles, mod_base=mod_base, tiles_per_seq=seq_len // tm,
                             final=final)
    return pl.pallas_call(
        kern,
        out_shape=jax.ShapeDtypeStruct((n, d), F32),
        grid=(n_tiles,),
        in_specs=[
            pl.BlockSpec((tm * TOP_K,), lambda i: (r0 + i,), memory_space=pltpu.SMEM),
            pl.BlockSpec((tm * TOP_K,), lambda i: (r0 + jnp.minimum(i + 1, n_tiles - 1),), memory_space=pltpu.SMEM),
            pl.BlockSpec((tm, d), lambda i: (i, 0)),
            pl.BlockSpec((tm, TOP_K), lambda i: (i, 0)),
            pl.BlockSpec((None, MOD_ROWS, d), lambda i: (layer, 0, 5)),
            pl.BlockSpec((1, d), lambda i: (0, 0)),
            pl.BlockSpec(memory_space=pl.ANY),
        ],
        out_specs=pl.BlockSpec((tm, d), lambda i: (i, 0)),
        scratch_shapes=[pltpu.VMEM((2, TOP_K, tm, d), F32), pltpu.SemaphoreType.DMA((2,))],
        compiler_params=_cparams("arbitrary"),
        name="moe_combine",
    )(pos_flat, pos_flat, x1, top_p, mod, final_w, ys)


MOE_TILE = 256


def _rope_tables(t):
    rows = t // GRID_W
    row = jnp.repeat(jnp.arange(rows, dtype=F32), GRID_W)
    col = jnp.tile(jnp.arange(GRID_W, dtype=F32), rows)
    n_freq = HEAD_DIM // 4
    inv = ROPE_THETA ** (-jnp.arange(n_freq, dtype=F32) / n_freq)
    ang = jnp.concatenate([row[:, None] * inv, col[:, None] * inv], axis=-1)
    cos = jnp.repeat(jnp.cos(ang), 2, axis=-1)
    sin = jnp.repeat(jnp.sin(ang), 2, axis=-1)
    even = (jnp.arange(HEAD_DIM) % 2 == 0)[None, :]
    return cos, jnp.where(even, -sin, 0.0), jnp.where(even, 0.0, sin)


def _layer_params(a, n_layers):
    d = D_MODEL
    w = MIX_W
    p = dict(a)
    for name in ("norm1_w", "norm2_w"):
        p[name] = a[name].reshape(n_layers, 1, d)
    for name in ("rwkv_a0", "rwkv_k_k", "rwkv_k_a", "rwkv_ln_w"):
        p[name] = a[name].reshape(n_layers, 1, w)
    p["rwkv_r_k"] = a["rwkv_r_k"].reshape(n_layers, 1, w)
    p["q_norm_w"] = a["q_norm_w"].reshape(n_layers, 1, HEAD_DIM)
    p["k_norm_w"] = a["k_norm_w"].reshape(n_layers, 1, HEAD_DIM)
    wr, ar, gr = a["rwkv_w_up"].shape[2], a["rwkv_a_up"].shape[1], a["rwkv_g_up"].shape[1]
    assert wr + ar + gr == RWKV_LORA
    p["w_up_pad"] = jnp.pad(a["rwkv_w_up"], ((0, 0), (0, 0), (0, ar + gr), (0, 0)))
    p["a_up_pad"] = jnp.pad(a["rwkv_a_up"], ((0, 0), (wr, gr), (0, 0)))
    p["g_up_pad"] = jnp.pad(a["rwkv_g_up"], ((0, 0), (wr + ar, 0), (0, 0)))
    head = jnp.arange(w, dtype=jnp.int32) // RWKV_N
    p["seg_ones"] = (head[:, None] == head[None, :]).astype(BF16)
    p["w_branch_bf"] = a["w_branch"].astype(BF16)
    p["w_out_bf"] = a["w_out"].astype(BF16)
    p["router_b"] = a["router_b"].reshape(n_layers, 1, N_EXPERTS)
    p["b_gate_up"] = a["b_gate_up"].reshape(n_layers, N_EXPERTS, 1, 2 * D_FF)
    p["b_down"] = a["b_down"].reshape(n_layers, N_EXPERTS, 1, d)
    return p


def _mixer(x, p, mod, layer, batch, seq_len, mod_base, latent_inputs):
    latent = latent_inputs is not None
    z = in_proj(x, p["norm1_w"], mod, p["w_in"], layer, 0, MAIN_COLS, seq_len=seq_len, mod_base=mod_base)
    zg = in_proj(x, p["norm1_w"], mod, p["w_in"], layer, MAIN_COLS, GATE_COLS, seq_len=seq_len, mod_base=mod_base)
    ops = rwkv_prep(z, p, layer, seq_len)
    if latent:
        rope, cache_k, cache_v, state_ret, state_rwkv = latent_inputs
        o_ret = retention(z, p["ret_decay"], layer, batch, seq_len, rope=rope, state0=state_ret)
        o_rwkv = rwkv_scan(ops, p["rwkv_ln_w"], layer, batch, seq_len, state0=state_rwkv)
        o_att = attention(z, p["q_norm_w"], p["k_norm_w"], layer, batch, seq_len,
                          rope=rope, cache_k=cache_k, cache_v=cache_v)
        ctx_out = None
    else:
        o_ret, s_ret = retention(z, p["ret_decay"], layer, batch, seq_len)
        o_rwkv, s_rwkv = rwkv_scan(ops, p["rwkv_ln_w"], layer, batch, seq_len)
        o_att, new_k, new_v = attention(z, p["q_norm_w"], p["k_norm_w"], layer, batch, seq_len)
        ctx_out = (new_k, new_v, s_ret, s_rwkv)
    merged = merge(x, o_ret, o_rwkv, o_att, zg, p, mod, layer, seq_len=seq_len, mod_base=mod_base)
    return merged, ctx_out


def kernel(x_prompt, x_sample, cache_k, cache_v, state_ret, state_rwkv, c, c_ctx, norm1_w, norm2_w, mod_w, mod_b, w_in, ret_decay, rwkv_mu, rwkv_w0, rwkv_w_up, rwkv_a0, rwkv_a_up, rwkv_g_up, rwkv_k_k, rwkv_k_a, rwkv_r_k, rwkv_ln_w, q_norm_w, k_norm_w, w_branch, w_out, router_w, router_b, w_gate_up, b_gate_up, w_down, b_down, final_norm_w):
    bp, tp, d = x_prompt.shape
    bs, ts, _ = x_sample.shape
    n_layers = w_in.shape[0]
    assert bs + 1 <= MOD_ROWS
    p = _layer_params(dict(
        norm1_w=norm1_w, norm2_w=norm2_w, w_in=w_in, ret_decay=ret_decay, rwkv_mu=rwkv_mu, rwkv_w0=rwkv_w0,
        rwkv_w_up=rwkv_w_up, rwkv_a0=rwkv_a0, rwkv_a_up=rwkv_a_up, rwkv_g_up=rwkv_g_up, rwkv_k_k=rwkv_k_k,
        rwkv_k_a=rwkv_k_a, rwkv_r_k=rwkv_r_k, rwkv_ln_w=rwkv_ln_w, q_norm_w=q_norm_w, k_norm_w=k_norm_w,
        w_branch=w_branch, w_out=w_out, router_w=router_w, router_b=router_b, w_gate_up=w_gate_up,
        b_gate_up=b_gate_up, w_down=w_down, b_down=b_down), n_layers)
    cond_rows = jnp.zeros((MOD_ROWS, d), F32).at[0].set(c_ctx).at[1:1 + bs].set(c)
    mod = modulation(cond_rows, mod_w, mod_b)
    rope = _rope_tables(ts)
    final_w = final_norm_w.reshape(1, d)
    np_, ns = bp * tp, bs * ts

    xp = x_prompt.reshape(np_, d)
    xs = x_sample.reshape(ns, d)
    ks_, vs_, srs, sws = [], [], [], []
    for layer in range(n_layers):
        (x1p, h2p, tip, tpp), (k_l, v_l, sr_l, sw_l) = _mixer(xp, p, mod, layer, bp, tp, 0, None)
        (x1s, h2s, tis, tps), _ = _mixer(xs, p, mod, layer, bs, ts, 1,
                                         (rope, cache_k, cache_v, state_ret, state_rwkv))
        ks_.append(k_l.reshape(bp, tp, ATT_KV_HEADS, HEAD_DIM))
        vs_.append(v_l.reshape(bp, tp, ATT_KV_HEADS, HEAD_DIM))
        srs.append(sr_l)
        sws.append(sw_l)
        h2 = jnp.concatenate([h2p, h2s], axis=0)
        pos, tile_expert, n_used, last_tile = route(jnp.concatenate([tip, tis], axis=0), MOE_TILE)
        pos_flat = pos.reshape(-1)
        xs_sorted = dispatch(h2, pos_flat, last_tile, n_used, MOE_TILE)
        ys = moe_experts(xs_sorted, tile_expert, n_used, p, layer, MOE_TILE)
        final = layer == n_layers - 1
        xp = combine(x1p, tpp, pos_flat, 0, ys, mod, final_w, layer, seq_len=tp, mod_base=0, final=final)
        xs = combine(x1s, tps, pos_flat, np_, ys, mod, final_w, layer, seq_len=ts, mod_base=1, final=final)
    return (xp.reshape(bp, tp, d), xs.reshape(bs, ts, d),
            jnp.stack(ks_, axis=1), jnp.stack(vs_, axis=1), jnp.stack(srs, axis=1), jnp.stack(sws, axis=1))
```

```python
import functools

import jax
import jax.numpy as jnp
from jax import lax
from jax.experimental import pallas as pl
from jax.experimental.pallas import tpu as pltpu

F32 = jnp.float32
BF16 = jnp.bfloat16
Z_DTYPE = BF16
Z_ROWS = 16

D_MODEL = 1024
DEPTH = 2
GRID_W = 64
MIX_W = D_MODEL // 2
N_BRANCH = 3
EPS = 1e-6
HEAD_DIM = 128
ATT_HEADS = MIX_W // HEAD_DIM
ATT_KV_HEADS = ATT_HEADS // 2
ROPE_THETA = 10000.0
RET_HEADS = MIX_W // HEAD_DIM
RET_CHUNK = 128
RET_UNROLL = 4
ATT_ROW_CHUNK = 128
RWKV_N = 64
RWKV_HEADS = MIX_W // RWKV_N
RWKV_CHUNK = 64
RWKV_GROUP = 4
RWKV_LORA = 256
RWKV_LN_EPS = 64e-5
N_EXPERTS = 32
TOP_K = 4
D_FF = D_MODEL
SWIGLU_LIMIT = 7.0
SWIGLU_ALPHA = 1.702
RET_COLS = 4 * MIX_W
RWKV_COLS = 3 * MIX_W + RWKV_LORA
ATT_COLS = (ATT_HEADS + 2 * ATT_KV_HEADS) * HEAD_DIM
GATE_COLS = N_BRANCH * D_MODEL
MAIN_COLS = RET_COLS + RWKV_COLS + ATT_COLS
N_MOD = 6
MOD_ROWS = 8

LANE = 128
SUBLANE = 8
VMEM_LIMIT = 56 * 1024 * 1024

_NT = (((1,), (1,)), ((), ()))
_TN = (((0,), (0,)), ((), ()))


def _cparams(*sem):
    return pltpu.CompilerParams(dimension_semantics=sem, vmem_limit_bytes=VMEM_LIMIT)


def _bdot(a, b):
    return jnp.dot(a.astype(BF16), b.astype(BF16), preferred_element_type=F32)


def _bdot_g(a, b, dims):
    return lax.dot_general(a.astype(BF16), b.astype(BF16), dims, preferred_element_type=F32)


def _split3(x):
    hi = x.astype(BF16)
    r1 = x - hi.astype(F32)
    mid = r1.astype(BF16)
    lo = (r1 - mid.astype(F32)).astype(BF16)
    return hi, mid, lo


def _dot_x3(a, b):
    ah = a.astype(BF16)
    bh = b.astype(BF16)
    al = (a - ah.astype(F32)).astype(BF16)
    bl = (b - bh.astype(F32)).astype(BF16)
    d = functools.partial(jnp.dot, preferred_element_type=F32)
    return d(ah, bh) + (d(ah, bl) + d(al, bh))


def _dot_exact_rhs(a, b_exact):
    hi, mid, lo = _split3(a)
    d = functools.partial(jnp.dot, preferred_element_type=F32)
    return d(hi, b_exact) + (d(mid, b_exact) + d(lo, b_exact))


def _sigmoid(x):
    return 1.0 / (1.0 + jnp.exp(-x))


def _silu(x):
    return x * _sigmoid(x)


def _mod_kernel(cond_ref, w_ref, b_ref, o_ref):
    s = _silu(cond_ref[...])
    o_ref[...] = _dot_x3(s, w_ref[...]) + b_ref[...]


def modulation(cond_rows, mod_w, mod_b):
    n_layers, d, cols = mod_w.shape
    tn = d
    return pl.pallas_call(
        _mod_kernel,
        out_shape=jax.ShapeDtypeStruct((n_layers, MOD_ROWS, cols), F32),
        grid=(n_layers, cols // tn),
        in_specs=[
            pl.BlockSpec((MOD_ROWS, d), lambda l, j: (0, 0)),
            pl.BlockSpec((None, d, tn), lambda l, j: (l, 0, j)),
            pl.BlockSpec((None, 1, tn), lambda l, j: (l, 0, j)),
        ],
        out_specs=pl.BlockSpec((None, MOD_ROWS, tn), lambda l, j: (l, 0, j)),
        compiler_params=_cparams("arbitrary", "arbitrary"),
        name="modulation",
    )(cond_rows, mod_w, mod_b.reshape(n_layers, 1, cols))


def _mod_row(i, mod_base, tiles_per_seq):
    if mod_base == 0:
        return 0
    return mod_base + i // tiles_per_seq


def _modnorm(x, nw, shift, scale):
    ms = jnp.mean(x * x, axis=-1, keepdims=True)
    return (x * lax.rsqrt(ms + EPS)) * nw * (1.0 + scale) + shift


def _in_proj_kernel(x_ref, nw_ref, sh_ref, sc_ref, w_ref, o_ref, h_scr, *, mod_base, tiles_per_seq):
    i = pl.program_id(0)

    @pl.when(pl.program_id(1) == 0)
    def _():
        r = _mod_row(i, mod_base, tiles_per_seq)
        h = _modnorm(x_ref[...], nw_ref[...], sh_ref[pl.ds(r, 1), :], sc_ref[pl.ds(r, 1), :])
        h_scr[...] = h.astype(BF16)

    o_ref[...] = jnp.dot(h_scr[...], w_ref[...].astype(BF16), preferred_element_type=F32).astype(o_ref.dtype)


def in_proj(x, norm_w, mod, w_in, layer, col0, ncols, *, seq_len, mod_base):
    n, d = x.shape
    tm = min(n, 2048, seq_len if mod_base else n)
    tn = 256
    assert n % tm == 0 and ncols % tn == 0 and col0 % tn == 0 and (mod_base == 0 or seq_len % tm == 0)
    j0 = col0 // tn
    kern = functools.partial(_in_proj_kernel, mod_base=mod_base, tiles_per_seq=seq_len // tm)
    return pl.pallas_call(
        kern,
        out_shape=jax.ShapeDtypeStruct((n, ncols), Z_DTYPE),
        grid=(n // tm, ncols // tn),
        in_specs=[
            pl.BlockSpec((tm, d), lambda i, j: (i, 0)),
            pl.BlockSpec((None, 1, d), lambda i, j: (layer, 0, 0)),
            pl.BlockSpec((None, MOD_ROWS, d), lambda i, j: (layer, 0, 0)),
            pl.BlockSpec((None, MOD_ROWS, d), lambda i, j: (layer, 0, 1)),
            pl.BlockSpec((None, d, tn), lambda i, j: (layer, 0, j0 + j)),
        ],
        out_specs=pl.BlockSpec((tm, tn), lambda i, j: (i, j)),
        scratch_shapes=[pltpu.VMEM((tm, d), BF16)],
        compiler_params=_cparams("arbitrary", "arbitrary"),
        name="in_proj",
    )(x, norm_w, mod, mod, w_in)


def _rope(x, cos, sin_even, sin_odd):
    nxt = pltpu.roll(x, LANE - 1, axis=1)
    prv = pltpu.roll(x, 1, axis=1)
    return x * cos + nxt * sin_even + prv * sin_odd


def _rms(x, w):
    return x * lax.rsqrt(jnp.mean(x * x, axis=-1, keepdims=True) + EPS) * w


def _iota2(shape, dim):
    return lax.broadcasted_iota(jnp.int32, shape, dim)


def _retention_kernel(*refs, layer, latent, n_chunks):
    if latent:
        (decay_ref, q_ref, k_ref, v_ref, g_ref, cos_ref, se_ref, so_ref, s0_ref,
         o_ref, q_scr, k_scr, kf_scr, kb_scr) = refs
    else:
        (decay_ref, q_ref, k_ref, v_ref, g_ref,
         o_ref, sout_ref, q_scr, k_scr, kf_scr, kb_scr) = refs
    c_len = RET_CHUNK
    h = pl.program_id(1)
    lgf = -decay_ref[layer, 0, h]
    lgb = -decay_ref[layer, 1, h]

    if latent:
        q_scr[...] = _rope(q_ref[...].astype(F32), cos_ref[...], se_ref[...], so_ref[...])
        k_scr[...] = _rope(k_ref[...].astype(F32), cos_ref[...], se_ref[...], so_ref[...]) * (HEAD_DIM ** -0.5)
    else:
        q_scr[...] = q_ref[...].astype(F32)
        k_scr[...] = k_ref[...].astype(F32) * (HEAD_DIM ** -0.5)

    ii = _iota2((c_len, c_len), 0)
    jj = _iota2((c_len, c_len), 1)
    diff = (ii - jj).astype(F32)
    tok = ii.astype(F32)
    dcomb = (jnp.where(diff >= 0, jnp.exp(lgf * jnp.maximum(diff, 0.0)), 0.0)
             + jnp.where(diff <= 0, jnp.exp(lgb * jnp.maximum(-diff, 0.0)), 0.0))
    xi_f = jnp.exp(lgf * (tok + 1.0))
    xi_b = jnp.exp(lgb * (c_len - tok))
    zeta_f = jnp.exp(lgf * (c_len - 1.0 - tok))
    zeta_b = jnp.exp(lgb * tok)
    gc_f = jnp.exp(jnp.full((c_len, c_len), c_len, F32) * lgf)
    gc_b = jnp.exp(jnp.full((c_len, c_len), c_len, F32) * lgb)

    def rows(c):
        return pl.ds(pl.multiple_of(c * c_len, c_len), c_len)

    unroll = min(RET_UNROLL, n_chunks)
    assert n_chunks % unroll == 0

    def local(it, carry):
        cs = [it * unroll + u for u in range(unroll)]
        kcs = [k_scr[rows(c), :] for c in cs]
        vcs = [v_ref[rows(c), :] for c in cs]
        for c, kc, vc in zip(cs, kcs, vcs):
            kf_scr[c] = _bdot_g(kc * zeta_f, vc, _TN)
        for c, kc, vc in zip(cs, kcs, vcs):
            kb_scr[c] = _bdot_g(kc * zeta_b, vc, _TN)
        return carry

    lax.fori_loop(0, n_chunks // unroll, local, 0)

    if latent:
        sf0 = s0_ref[0]
        sb0 = s0_ref[1]
    else:
        sf0 = jnp.zeros((c_len, c_len), F32)
        sb0 = sf0

    def scan_f(c, s):
        kl = kf_scr[c]
        kf_scr[c] = s
        return s * gc_f + kl

    def scan_b(t, s):
        c = n_chunks - 1 - t
        kl = kb_scr[c]
        kb_scr[c] = s
        return s * gc_b + kl

    sf = lax.fori_loop(0, n_chunks, scan_f, sf0)
    sb = lax.fori_loop(0, n_chunks, scan_b, sb0)
    if not latent:
        sout_ref[0] = sf
        sout_ref[1] = sb

    def output(it, carry):
        cs = [it * unroll + u for u in range(unroll)]
        qcs = [q_scr[rows(c), :] for c in cs]
        atts = [_bdot_g(qc, k_scr[rows(c), :], _NT) * dcomb for c, qc in zip(cs, qcs)]
        inter = [_bdot(jnp.concatenate([qc * xi_f, qc * xi_b], axis=1),
                       jnp.concatenate([kf_scr[c], kb_scr[c]], axis=0)) for c, qc in zip(cs, qcs)]
        os_ = [_bdot(att, v_ref[rows(c), :]) + x for c, att, x in zip(cs, atts, inter)]
        mus = [jnp.mean(o, axis=-1, keepdims=True) for o in os_]
        ds = [o - mu for o, mu in zip(os_, mus)]
        vars_ = [jnp.mean(d * d, axis=-1, keepdims=True) for d in ds]
        for c, d, var in zip(cs, ds, vars_):
            o_ref[rows(c), :] = _silu(g_ref[rows(c), :].astype(F32)) * (d * lax.rsqrt(var + EPS))
        return carry

    lax.fori_loop(0, n_chunks // unroll, output, 0)


def retention(z, ret_decay, layer, batch, seq_len, *, rope=None, state0=None):
    latent = rope is not None
    n = z.shape[0]
    t = seq_len
    n_chunks = t // RET_CHUNK
    hd = HEAD_DIM
    nh = RET_HEADS
    col = lambda base: pl.BlockSpec((t, hd), lambda b, h, base=base: (b, base + h))
    in_specs = [pl.BlockSpec(memory_space=pltpu.SMEM), col(0), col(nh), col(2 * nh), col(3 * nh)]
    args = [ret_decay, z, z, z, z]
    out_shape = [jax.ShapeDtypeStruct((n, MIX_W), F32)]
    out_specs = [pl.BlockSpec((t, hd), lambda b, h: (b, h))]
    if latent:
        tab = pl.BlockSpec((t, hd), lambda b, h: (0, 0))
        in_specs += [tab, tab, tab,
                     pl.BlockSpec((None, None, 2, None, hd, hd), lambda b, h: (b, layer, 0, h, 0, 0))]
        args += [*rope, state0]
    else:
        out_shape.append(jax.ShapeDtypeStruct((batch, 2, nh, hd, hd), F32))
        out_specs.append(pl.BlockSpec((None, 2, None, hd, hd), lambda b, h: (b, 0, h, 0, 0)))
    kern = functools.partial(_retention_kernel, layer=layer, latent=latent, n_chunks=n_chunks)
    res = pl.pallas_call(
        kern,
        out_shape=out_shape,
        grid=(batch, nh),
        in_specs=in_specs,
        out_specs=out_specs,
        scratch_shapes=[pltpu.VMEM((t, hd), F32), pltpu.VMEM((t, hd), F32),
                        pltpu.VMEM((n_chunks, hd, hd), F32), pltpu.VMEM((n_chunks, hd, hd), F32)],
        compiler_params=_cparams("arbitrary", "arbitrary"),
        name="retention",
    )(*args)
    return res if not latent else res[0]


def _attention_kernel(*refs, latent, tq, past):
    if latent:
        (q_ref, k_ref, v_ref, qn_ref, kn_ref, cos_ref, se_ref, so_ref, ck_ref, cv_ref,
         o_ref, kb_scr, vb_scr) = refs
    else:
        (q_ref, k_ref, v_ref, qn_ref, kn_ref,
         o_ref, nk_ref, nv_ref, kb_scr, vb_scr) = refs
    qi = pl.program_id(2)
    hd = HEAD_DIM

    @pl.when(qi == 0)
    def _():
        kn = _rms(k_ref[...].astype(F32), kn_ref[...])
        if latent:
            kb_scr[0:past, :] = ck_ref[...].astype(BF16)
            vb_scr[0:past, :] = cv_ref[...].astype(BF16)
            kb_scr[past:, :] = _rope(kn, cos_ref[...], se_ref[...], so_ref[...]).astype(BF16)
            vb_scr[past:, :] = v_ref[...].astype(BF16)
        else:
            kb_scr[...] = kn.astype(BF16)
            vb_scr[...] = v_ref[...].astype(BF16)
            nk_ref[...] = kn
            nv_ref[...] = v_ref[...].astype(F32)

    q = q_ref[...].astype(F32)
    heads = []
    for g in range(2):
        qg = _rms(q[:, g * hd:(g + 1) * hd], qn_ref[...])
        if latent:
            r = pl.ds(pl.multiple_of(qi * tq, tq), tq)
            qg = _rope(qg, cos_ref[r, :], se_ref[r, :], so_ref[r, :])
        heads.append(qg)
    qq = jnp.concatenate(heads, axis=0).astype(BF16)
    rc = ATT_ROW_CHUNK
    n_rc = (2 * tq) // rc
    scale = hd ** -0.5
    scores, probs, outs = {}, {}, {}

    def score(c):
        scores[c] = lax.dot_general(qq[c * rc:(c + 1) * rc], kb_scr[...], _NT, preferred_element_type=F32)

    def soft(c):
        s = scores.pop(c)
        p = jnp.exp((s - jnp.max(s, axis=-1, keepdims=True)) * scale)
        probs[c] = (p.astype(BF16), jnp.sum(p, axis=-1, keepdims=True))

    def out(c):
        p, l = probs.pop(c)
        outs[c] = jnp.dot(p, vb_scr[...], preferred_element_type=F32) / l

    score(0)
    for c in range(n_rc):
        if c + 1 < n_rc:
            score(c + 1)
        soft(c)
        if c >= 1:
            out(c - 1)
    out(n_rc - 1)
    o = jnp.concatenate([outs[c] for c in range(n_rc)], axis=0)
    o_ref[...] = jnp.concatenate([o[:tq], o[tq:]], axis=1)


def attention(z, q_norm_w, k_norm_w, layer, batch, seq_len, *, rope=None, cache_k=None, cache_v=None):
    latent = rope is not None
    n = z.shape[0]
    t = seq_len
    hd = HEAD_DIM
    tq = min(t, 256)
    past = cache_k.shape[2] if latent else 0
    q0 = (RET_COLS + RWKV_COLS) // (2 * hd)
    k0 = (RET_COLS + RWKV_COLS) // hd + ATT_HEADS
    v0 = k0 + ATT_KV_HEADS
    nq = t // tq
    in_specs = [
        pl.BlockSpec((tq, 2 * hd), lambda b, g, i: (b * nq + i, q0 + g)),
        pl.BlockSpec((t, hd), lambda b, g, i: (b, k0 + g)),
        pl.BlockSpec((t, hd), lambda b, g, i: (b, v0 + g)),
        pl.BlockSpec((None, 1, hd), lambda b, g, i: (layer, 0, 0)),
        pl.BlockSpec((None, 1, hd), lambda b, g, i: (layer, 0, 0)),
    ]
    args = [z, z, z, q_norm_w, k_norm_w]
    out_shape = [jax.ShapeDtypeStruct((n, MIX_W), F32)]
    out_specs = [pl.BlockSpec((tq, 2 * hd), lambda b, g, i: (b * nq + i, g))]
    if latent:
        tab = pl.BlockSpec((t, hd), lambda b, g, i: (0, 0))
        cache = pl.BlockSpec((None, None, past, hd), lambda b, g, i: (b, layer, 0, g))
        in_specs += [tab, tab, tab, cache, cache]
        flat = cache_k.shape[:3] + (ATT_KV_HEADS * hd,)
        args += [*rope, cache_k.reshape(flat), cache_v.reshape(flat)]
    else:
        new = jax.ShapeDtypeStruct((batch, t, ATT_KV_HEADS * hd), F32)
        out_shape += [new, new]
        spec = pl.BlockSpec((None, t, hd), lambda b, g, i: (b, 0, g))
        out_specs += [spec, spec]
    kern = functools.partial(_attention_kernel, latent=latent, tq=tq, past=past)
    res = pl.pallas_call(
        kern,
        out_shape=out_shape,
        grid=(batch, ATT_KV_HEADS, nq),
        in_specs=in_specs,
        out_specs=out_specs,
        scratch_shapes=[pltpu.VMEM((past + t, hd), BF16), pltpu.VMEM((past + t, hd), BF16)],
        compiler_params=_cparams("arbitrary", "arbitrary", "arbitrary"),
        name="attention",
    )(*args)
    return res if not latent else res[0]


def _softplus(x):
    return jnp.maximum(x, 0.0) + jnp.log1p(jnp.exp(-jnp.abs(x)))


def _token_shift(u, prev_row, next_row, mu):
    tr = u.shape[0]
    row = _iota2(u.shape, 0)
    prev = jnp.where(row == 0, prev_row, pltpu.roll(u, 1, axis=0))
    nxt = jnp.where(row == tr - 1, next_row, pltpu.roll(u, tr - 1, axis=0))
    return u + mu[0:1, :] * (prev - u) + mu[1:2, :] * (nxt - u)


def _rwkv_prep_kernel(zr, zk, zv, zl, pr, pk, pv, pl_, nr, nk, nv, nl, mu_r, mu_k, mu_v, mu_l,
                      w0_ref, wup_ref, a0_ref, aup_ref, gup_ref, kkw_ref, kaw_ref, rkw_ref, seg_ref,
                      r_o, k_o, v_o, kk_o, b_o, lw0_o, lw1_o, bonus_o, gate_o, *, tiles_per_seq):
    i = pl.program_id(0)
    pos = i % tiles_per_seq
    keep_prev = jnp.where(pos == 0, 0.0, 1.0)
    keep_next = jnp.where(pos == tiles_per_seq - 1, 0.0, 1.0)

    def shifted(cur, prv, nxt, mu):
        return _token_shift(cur[...].astype(F32), prv[Z_ROWS - 1:Z_ROWS, :].astype(F32) * keep_prev,
                            nxt[0:1, :].astype(F32) * keep_next, mu[...])

    wr = shifted(zr, pr, nr, mu_r)
    wk = shifted(zk, pk, nk, mu_k)
    wv = shifted(zv, pv, nv, mu_v)
    lo = shifted(zl, pl_, nl, mu_l)

    seg = seg_ref[...]
    icl = _sigmoid(a0_ref[...] + _dot_x3(lo, aup_ref[...]))
    gate_o[...] = _dot_x3(_sigmoid(lo), gup_ref[...])
    th = jnp.tanh(lo)
    for d, lw_o in enumerate((lw0_o, lw1_o)):
        wl = w0_ref[d:d + 1, :] + _dot_x3(th, wup_ref[d])
        lw_o[...] = -jnp.exp(-_softplus(-wl) - 0.5)
    kk = wk * kkw_ref[...]
    kk = kk * lax.rsqrt(_dot_exact_rhs(kk * kk, seg) + 1e-12)
    k = wk * (1.0 + (icl - 1.0) * kaw_ref[...])
    r_o[...] = wr
    k_o[...] = k
    v_o[...] = wv
    kk_o[...] = kk
    b_o[...] = kk * icl
    bonus_o[...] = _dot_exact_rhs(wr * k * rkw_ref[...], seg) * wv


def rwkv_prep(z, p, layer, seq_len):
    n = z.shape[0]
    tr = 256
    w = MIX_W
    c0 = RET_COLS // w
    cl = (RET_COLS + 3 * w) // RWKV_LORA
    nt = n // tr
    rb = tr // Z_ROWS
    last = n // Z_ROWS - 1
    cur = lambda width, c: pl.BlockSpec((tr, width), lambda i, c=c: (i, c))
    prv = lambda width, c: pl.BlockSpec((Z_ROWS, width), lambda i, c=c: (jnp.maximum(i * rb - 1, 0), c))
    nxt = lambda width, c: pl.BlockSpec((Z_ROWS, width), lambda i, c=c: (jnp.minimum((i + 1) * rb, last), c))
    cols = [(w, c0), (w, c0 + 1), (w, c0 + 2), (RWKV_LORA, cl)]
    mu_cols = [(w, 0), (w, 1), (w, 2), (RWKV_LORA, 3 * w // RWKV_LORA)]
    whole = lambda *shape: pl.BlockSpec((None,) + shape, lambda i: (layer,) + (0,) * len(shape))
    in_specs = ([cur(*c) for c in cols] + [prv(*c) for c in cols] + [nxt(*c) for c in cols]
                + [pl.BlockSpec((None, 2, width), lambda i, c=c: (layer, 0, c)) for width, c in mu_cols]
                + [whole(2, w), whole(2, RWKV_LORA, w), whole(1, w), whole(RWKV_LORA, w), whole(RWKV_LORA, w),
                   whole(1, w), whole(1, w), whole(1, w), pl.BlockSpec((w, w), lambda i: (0, 0))])
    out = jax.ShapeDtypeStruct((n, w), F32)
    kern = functools.partial(_rwkv_prep_kernel, tiles_per_seq=seq_len // tr)
    return pl.pallas_call(
        kern,
        out_shape=[out] * 9,
        grid=(nt,),
        in_specs=in_specs,
        out_specs=[pl.BlockSpec((tr, w), lambda i: (i, 0))] * 9,
        compiler_params=_cparams("arbitrary"),
        name="rwkv_prep",
    )(*([z] * 12), *([p["rwkv_mu"]] * 4), p["rwkv_w0"], p["w_up_pad"], p["rwkv_a0"], p["a_up_pad"], p["g_up_pad"],
      p["rwkv_k_k"], p["rwkv_k_a"], p["rwkv_r_k"], p["seg_ones"])


def _tri_inverse_steps(mats, eye, blk16, off32, off64):
    ds = [jnp.where(blk16, a, 0.0) for a in mats]
    ts = [eye + d for d in ds]
    pws = ds
    for _ in range(3):
        pws = [_bdot(pw, pw) for pw in pws]
        yield None
        ts = [_bdot(t, eye + pw) for t, pw in zip(ts, pws)]
        yield None
    for off in (off32, off64):
        halves = [_bdot(t, jnp.where(off, a, 0.0)) for t, a in zip(ts, mats)]
        yield None
        ts = [t + _bdot(h, t) for t, h in zip(ts, halves)]
        yield None
    yield ts


def _rwkv_scan_kernel(*refs, latent, n_chunks, group):
    scratch = refs[-10:]
    yf_scr, yb_scr, s_scr, ar_s, tinv_s, av_s, mrbk_s, vst_s, bko_s, gt_s = scratch
    if latent:
        (r_ref, k_ref, v_ref, kk_ref, b_ref, lw0_ref, lw1_ref, bonus_ref, gate_ref, lnw_ref, s0_ref,
         o_ref) = refs[:-10]
    else:
        (r_ref, k_ref, v_ref, kk_ref, b_ref, lw0_ref, lw1_ref, bonus_ref, gate_ref, lnw_ref,
         o_ref, sout_ref) = refs[:-10]
    cl = RWKV_CHUNK
    hn = RWKV_N
    two = 2 * cl
    m0 = _iota2((1, LANE), 1) < hn
    row = _iota2((two, two), 0)
    col = _iota2((two, two), 1)
    same = (row >= cl) == (col >= cl)
    eye = jnp.where(row == col, 1.0, 0.0)
    blk16 = (row >> 4) == (col >> 4)
    off32 = ((row >> 5) == (col >> 5)) & ~blk16
    off64 = same & ((row >> 5) != (col >> 5))
    r64 = _iota2((cl, cl), 0)
    c64 = _iota2((cl, cl), 1)
    masks = (
        (same & (row >= col), same & (row > col), jnp.where(r64 >= c64, 1.0, 0.0).astype(BF16)),
        (same & (row <= col), same & (row < col), jnp.where(r64 <= c64, 1.0, 0.0).astype(BF16)),
    )

    def stack(x):
        return jnp.concatenate([jnp.where(m0, x, 0.0), jnp.where(m0, 0.0, x)], axis=0)

    zero = jnp.zeros((hn, hn), F32)
    for d in range(2):
        if latent:
            s_scr[d] = jnp.concatenate(
                [jnp.concatenate([s0_ref[d, 0], zero], axis=1), jnp.concatenate([zero, s0_ref[d, 1]], axis=1)], axis=0)
        else:
            s_scr[d] = jnp.zeros((LANE, LANE), F32)

    lw_refs = (lw0_ref, lw1_ref)
    y_scrs = (yf_scr, yb_scr)

    def chunk_rows(g, j, d):
        ci = g * group + j
        ci = ci if d == 0 else n_chunks - 1 - ci
        return pl.ds(pl.multiple_of(ci * cl, cl), cl)

    units = [(j, d) for j in range(group) for d in range(2)]
    dot = functools.partial(jnp.dot, preferred_element_type=F32)

    def prepare_steps(g, slot):
        rows = [chunk_rows(g, j, d) for j, d in units]
        lws = [lw_refs[d][r, :] for (j, d), r in zip(units, rows)]
        parts = [_split3(lw) for lw in lws]
        lgs = [dot(masks[d][2], hi) + (dot(masks[d][2], mid) + dot(masks[d][2], lo))
               for (j, d), (hi, mid, lo) in zip(units, parts)]
        yield
        tots = [lg[cl - 1:cl, :] if d == 0 else lg[0:1, :] for (j, d), lg in zip(units, lgs)]
        ars, bks = [], []
        for i, r in enumerate(rows):
            lg, lw, tot = lgs[i], lws[i], tots[i]
            e_neg = jnp.exp(-lg)
            e_out = jnp.exp(tot - lg)
            bb = b_ref[r, :]
            kc = k_ref[r, :]
            ar = jnp.concatenate([stack(-kk_ref[r, :] * jnp.exp(lg - lw)), stack(r_ref[r, :] * jnp.exp(lg))], axis=0)
            ars.append(ar)
            bks.append(jnp.concatenate([stack(bb * e_neg), stack(kc * e_neg)], axis=0))
            ar_s[slot, i] = ar.astype(BF16)
            bko_s[slot, i] = jnp.concatenate([stack(bb * e_out), stack(kc * e_out)], axis=0).astype(BF16)
            gt_s[slot, i] = jnp.broadcast_to(jnp.exp(tot), (SUBLANE, LANE))
        yield
        crosses = [_bdot_g(ar, bk, _NT) for ar, bk in zip(ars, bks)]
        yield
        a_abs, a_aks = [], []
        for i, ((j, d), cross) in enumerate(zip(units, crosses)):
            incl, strict, _ = masks[d]
            a_abs.append(jnp.where(strict, cross[:two, :two], 0.0))
            a_aks.append(jnp.where(strict, cross[:two, two:], 0.0))
            mrbk_s[slot, i] = jnp.concatenate([jnp.where(incl, cross[two:, :two], 0.0),
                                               jnp.where(incl, cross[two:, two:], 0.0)], axis=1).astype(BF16)
        v_sts = [stack(v_ref[r, :]) for r in rows]
        for i, (a_ak, v_st) in enumerate(zip(a_aks, v_sts)):
            vst_s[slot, i] = v_st.astype(BF16)
            av_s[slot, i] = _bdot(a_ak, v_st)
        yield
        inverses = None
        for inverses in _tri_inverse_steps(a_abs, eye, blk16, off32, off64):
            yield
        for i, t_inv in enumerate(inverses):
            tinv_s[slot, i] = t_inv.astype(BF16)

    def advance_steps(g, slot):
        for j in range(group):
            dirs = [(2 * j + d, d, chunk_rows(g, j, d)) for d in range(2)]
            ss = [s_scr[d] for i, d, r in dirs]
            arss = [_bdot_g(ar_s[slot, i], s, _NT) for (i, d, r), s in zip(dirs, ss)]
            us = [dot(tinv_s[slot, i], (ars[:two] + av_s[slot, i]).astype(BF16)) for (i, d, r), ars in zip(dirs, arss)]
            uvs = [jnp.concatenate([u.astype(BF16), vst_s[slot, i]], axis=0) for (i, d, r), u in zip(dirs, us)]
            for (i, d, r), s, ars, uv in zip(dirs, ss, arss, uvs):
                y = ars[two:] + dot(mrbk_s[slot, i], uv)
                s_scr[d] = s * gt_s[slot, i][0:1, :] + lax.dot_general(uv, bko_s[slot, i], _TN,
                                                                       preferred_element_type=F32)
                y_scrs[d][r, :] = y[:cl] + y[cl:]
            yield

    def run(main, side=(), every=1):
        side = iter(side)
        for n_done, _ in enumerate(main, start=1):
            if n_done % every == 0:
                next(side, None)
        for _ in side:
            pass

    n_groups = n_chunks // group
    run(prepare_steps(0, 0))
    if n_groups > 1:
        def body(g, carry):
            run(prepare_steps(g, g % 2), advance_steps(g - 1, (g - 1) % 2), every=3)
            return carry

        lax.fori_loop(1, n_groups, body, 0)
    run(advance_steps(n_groups - 1, (n_groups - 1) % 2))

    if not latent:
        for d in range(2):
            s = s_scr[d]
            sout_ref[d, 0] = s[:hn, :hn]
            sout_ref[d, 1] = s[hn:, hn:]

    rt = 256 if (n_chunks * cl) % 256 == 0 else n_chunks * cl

    def epilogue(j, carry):
        rows = pl.ds(pl.multiple_of(j * rt, rt), rt)
        y = yf_scr[rows, :] + yb_scr[rows, :]
        s_all = jnp.sum(y, axis=-1, keepdims=True)
        s_0 = jnp.sum(jnp.where(m0, y, 0.0), axis=-1, keepdims=True)
        mu = jnp.where(m0, s_0, s_all - s_0) * (1.0 / hn)
        dlt = y - mu
        q_all = jnp.sum(dlt * dlt, axis=-1, keepdims=True)
        q_0 = jnp.sum(jnp.where(m0, dlt * dlt, 0.0), axis=-1, keepdims=True)
        var = jnp.where(m0, q_0, q_all - q_0) * (1.0 / hn)
        yn = dlt * lax.rsqrt(var + RWKV_LN_EPS) * lnw_ref[...] + bonus_ref[rows, :]
        o_ref[rows, :] = yn * gate_ref[rows, :]
        return carry

    lax.fori_loop(0, (n_chunks * cl) // rt, epilogue, 0)


def rwkv_scan(ops, ln_w, layer, batch, seq_len, *, state0=None):
    latent = state0 is not None
    n = ops[0].shape[0]
    t = seq_len
    hn = RWKV_N
    npair = RWKV_HEADS // 2
    blk = pl.BlockSpec((t, LANE), lambda b, h: (b, h))
    in_specs = [blk] * 9 + [pl.BlockSpec((None, 1, LANE), lambda b, h: (layer, 0, h))]
    args = [*ops, ln_w]
    out_shape = [jax.ShapeDtypeStruct((n, MIX_W), F32)]
    out_specs = [blk]
    if latent:
        in_specs.append(pl.BlockSpec((None, None, 2, 2, hn, hn), lambda b, h: (b, layer, 0, h, 0, 0)))
        args.append(state0)
    else:
        out_shape.append(jax.ShapeDtypeStruct((batch, 2, RWKV_HEADS, hn, hn), F32))
        out_specs.append(pl.BlockSpec((None, 2, 2, hn, hn), lambda b, h: (b, 0, h, 0, 0)))
    n_chunks = t // RWKV_CHUNK
    group = min(RWKV_GROUP, n_chunks)
    assert n_chunks % group == 0
    units = 2 * group
    two = 2 * RWKV_CHUNK
    kern = functools.partial(_rwkv_scan_kernel, latent=latent, n_chunks=n_chunks, group=group)
    res = pl.pallas_call(
        kern,
        out_shape=out_shape,
        grid=(batch, npair),
        in_specs=in_specs,
        out_specs=out_specs,
        scratch_shapes=[
            pltpu.VMEM((t, LANE), F32), pltpu.VMEM((t, LANE), F32), pltpu.VMEM((2, LANE, LANE), F32),
            pltpu.VMEM((2, units, 2 * two, LANE), BF16),
            pltpu.VMEM((2, units, two, two), BF16),
            pltpu.VMEM((2, units, two, LANE), F32),
            pltpu.VMEM((2, units, two, 2 * two), BF16),
            pltpu.VMEM((2, units, two, LANE), BF16),
            pltpu.VMEM((2, units, 2 * two, LANE), BF16),
            pltpu.VMEM((2, units, SUBLANE, LANE), F32),
        ],
        compiler_params=_cparams("arbitrary", "arbitrary"),
        name="rwkv_scan",
    )(*args)
    return res if not latent else res[0]


assert D_MODEL == SUBLANE * LANE


def _token_tiles(n_tok, index_map):
    return pl.BlockSpec((n_tok * SUBLANE, LANE), lambda *a: (index_map(*a), 0))


def _token_rows(t):
    return pl.ds(pl.multiple_of(t * SUBLANE, SUBLANE), SUBLANE)


def _load_token_tiles(ref, n_tok, lead=()):
    return jnp.concatenate([ref[(*lead, pl.ds(j, n_tok, stride=SUBLANE), slice(None))] for j in range(SUBLANE)],
                           axis=1)


def _store_token_tiles(ref, x):
    for j in range(SUBLANE):
        ref[pl.ds(j, x.shape[0], stride=SUBLANE), :] = x[:, j * LANE:(j + 1) * LANE]


def _topk_softmax(logits):
    lane = _iota2(logits.shape, 1)
    out_lane = _iota2((logits.shape[0], TOP_K), 1)
    n_exp = logits.shape[1]
    vals, idxs = [], []
    work = logits
    for _ in range(TOP_K):
        m = jnp.max(work, axis=-1, keepdims=True)
        idx = jnp.min(jnp.where(work == m, lane, n_exp), axis=-1, keepdims=True)
        vals.append(m)
        idxs.append(idx)
        work = jnp.where(lane == idx, -jnp.inf, work)
    es = [jnp.exp(v - vals[0]) for v in vals]
    denom = es[0] + es[1] + es[2] + es[3]
    top_i = jnp.zeros((logits.shape[0], TOP_K), jnp.int32)
    top_p = jnp.zeros((logits.shape[0], TOP_K), F32)
    for j in range(TOP_K):
        top_i = jnp.where(out_lane == j, idxs[j], top_i)
        top_p = jnp.where(out_lane == j, es[j] / denom, top_p)
    return top_i, top_p


def _merge_kernel(x_ref, oret_ref, orwkv_ref, oatt_ref, zg_ref, wb_ref, wo_ref, g1_ref, nw2_ref, sh2_ref, sc2_ref,
                  rw_ref, rb_ref, x1_ref, h2_ref, ti_ref, tp_ref, *, mod_base, tiles_per_seq):
    r = _mod_row(pl.program_id(0), mod_base, tiles_per_seq)
    d = x_ref.shape[1]
    acc = None
    for nb, o_ref in enumerate((oret_ref, orwkv_ref, oatt_ref)):
        proj = jnp.dot(o_ref[...].astype(BF16), wb_ref[nb], preferred_element_type=F32)
        term = _sigmoid(zg_ref[:, nb * d:(nb + 1) * d].astype(F32)) * proj
        acc = term if acc is None else acc + term
    out = jnp.dot(acc.astype(BF16), wo_ref[...], preferred_element_type=F32)
    x1 = x_ref[...] + g1_ref[pl.ds(r, 1), :] * out
    x1_ref[...] = x1
    h2 = _modnorm(x1, nw2_ref[...], sh2_ref[pl.ds(r, 1), :], sc2_ref[pl.ds(r, 1), :])
    _store_token_tiles(h2_ref, h2)
    logits = _dot_x3(h2, rw_ref[...]) + rb_ref[...]
    top_i, top_p = _topk_softmax(logits)
    ti_ref[...] = top_i
    tp_ref[...] = top_p


def merge(x, o_ret, o_rwkv, o_att, zg, p, mod, layer, *, seq_len, mod_base):
    n, d = x.shape
    tm = 256
    w = MIX_W
    row = lambda width: pl.BlockSpec((tm, width), lambda i: (i, 0))
    modc = lambda c: pl.BlockSpec((None, MOD_ROWS, d), lambda i, c=c: (layer, 0, c))
    kern = functools.partial(_merge_kernel, mod_base=mod_base, tiles_per_seq=seq_len // tm)
    return pl.pallas_call(
        kern,
        out_shape=[jax.ShapeDtypeStruct((n, d), F32), jax.ShapeDtypeStruct((n * SUBLANE, LANE), F32),
                   jax.ShapeDtypeStruct((n, TOP_K), jnp.int32), jax.ShapeDtypeStruct((n, TOP_K), F32)],
        grid=(n // tm,),
        in_specs=[
            row(d), row(w), row(w), row(w), row(N_BRANCH * d),
            pl.BlockSpec((None, N_BRANCH, w, d), lambda i: (layer, 0, 0, 0)),
            pl.BlockSpec((None, d, d), lambda i: (layer, 0, 0)),
            modc(2),
            pl.BlockSpec((None, 1, d), lambda i: (layer, 0, 0)),
            modc(3), modc(4),
            pl.BlockSpec((None, d, N_EXPERTS), lambda i: (layer, 0, 0)),
            pl.BlockSpec((None, 1, N_EXPERTS), lambda i: (layer, 0, 0)),
        ],
        out_specs=[row(d), _token_tiles(tm, lambda i: i), row(TOP_K), row(TOP_K)],
        compiler_params=_cparams("arbitrary"),
        name="merge",
    )(x, o_ret, o_rwkv, o_att, zg, p["w_branch_bf"], p["w_out_bf"], mod, p["norm2_w"], mod, mod,
      p["router_w"], p["router_b"])


def _moe_kernel(te_ref, nu_ref, xs_ref, wgu_ref, bgu_ref, wdn_ref, bdn_ref, y_ref, wgu_scr, wdn_scr):
    i = pl.program_id(0)
    prev = te_ref[jnp.maximum(i - 1, 0)]

    @pl.when((i == 0) | (te_ref[i] != prev))
    def _():
        wgu_scr[...] = wgu_ref[...].astype(BF16)
        wdn_scr[...] = wdn_ref[...].astype(BF16)

    @pl.when(i >= nu_ref[0])
    def _():
        y_ref[...] = jnp.zeros_like(y_ref)

    @pl.when(i < nu_ref[0])
    def _():
        f = wdn_scr.shape[0]
        x = _load_token_tiles(xs_ref, xs_ref.shape[0] // SUBLANE).astype(BF16)
        gu = jnp.dot(x, wgu_scr[...], preferred_element_type=F32) + bgu_ref[...]
        gate = jnp.minimum(gu[:, :f], SWIGLU_LIMIT)
        up = jnp.clip(gu[:, f:], -SWIGLU_LIMIT, SWIGLU_LIMIT)
        act = (up + 1.0) * gate * _sigmoid(SWIGLU_ALPHA * gate)
        y = jnp.dot(act.astype(BF16), wdn_scr[...], preferred_element_type=F32) + bdn_ref[...]
        _store_token_tiles(y_ref, y)


def moe_experts(xs, tile_expert, n_used, p, layer, tile):
    s = xs.shape[0] // SUBLANE
    d = D_MODEL
    f = D_FF
    grid_spec = pltpu.PrefetchScalarGridSpec(
        num_scalar_prefetch=2,
        grid=(s // tile,),
        in_specs=[
            _token_tiles(tile, lambda i, te, nu: jnp.minimum(i, nu[0] - 1)),
            pl.BlockSpec((None, None, d, 2 * f), lambda i, te, nu: (layer, te[i], 0, 0)),
            pl.BlockSpec((None, None, 1, 2 * f), lambda i, te, nu: (layer, te[i], 0, 0)),
            pl.BlockSpec((None, None, f, d), lambda i, te, nu: (layer, te[i], 0, 0)),
            pl.BlockSpec((None, None, 1, d), lambda i, te, nu: (layer, te[i], 0, 0)),
        ],
        out_specs=_token_tiles(tile, lambda i, te, nu: i),
        scratch_shapes=[pltpu.VMEM((d, 2 * f), BF16), pltpu.VMEM((f, d), BF16)],
    )
    return pl.pallas_call(
        _moe_kernel,
        out_shape=jax.ShapeDtypeStruct((s * SUBLANE, LANE), F32),
        grid_spec=grid_spec,
        compiler_params=_cparams("arbitrary"),
        name="moe_experts",
    )(tile_expert, n_used, xs, p["w_gate_up"], p["b_gate_up"], p["w_down"], p["b_down"])


def _choice_onehots(top_i):
    lane = _iota2((top_i.shape[0], N_EXPERTS), 1)
    return [jnp.where(top_i[:, k:k + 1] == lane, 1.0, 0.0) for k in range(TOP_K)]


def _per_choice(onehots, table):
    out_lane = _iota2((onehots[0].shape[0], TOP_K), 1)
    out = jnp.zeros((onehots[0].shape[0], TOP_K), F32)
    for k, oh in enumerate(onehots):
        out = jnp.where(out_lane == k, jnp.sum(oh * table, axis=-1, keepdims=True), out)
    return out


def _route_rank_kernel(ti_ref, rank_ref, cnt_ref, run_scr):
    @pl.when(pl.program_id(0) == 0)
    def _():
        run_scr[...] = jnp.zeros_like(run_scr)

    tm = ti_ref.shape[0]
    ohs = _choice_onehots(ti_ref[...])
    chosen = ohs[0] + ohs[1] + ohs[2] + ohs[3]
    earlier = jnp.where(_iota2((tm, tm), 0) > _iota2((tm, tm), 1), 1.0, 0.0).astype(BF16)
    before = jnp.dot(earlier, chosen.astype(BF16), preferred_element_type=F32) + run_scr[...]
    rank_ref[...] = _per_choice(ohs, before).astype(jnp.int32)
    run_scr[...] += jnp.sum(chosen, axis=0, keepdims=True)
    cnt_ref[...] = run_scr[...].astype(jnp.int32)


def _route_pos_kernel(ti_ref, rank_ref, off_ref, pos_ref):
    ohs = _choice_onehots(ti_ref[...])
    pos_ref[...] = rank_ref[...] + _per_choice(ohs, off_ref[...]).astype(jnp.int32)


def route(top_i, tile):
    n = top_i.shape[0]
    tm = 256
    tok = pl.BlockSpec((tm, TOP_K), lambda i: (i, 0))
    per_e = pl.BlockSpec((1, N_EXPERTS), lambda i: (0, 0))
    rank, counts = pl.pallas_call(
        _route_rank_kernel,
        out_shape=[jax.ShapeDtypeStruct((n, TOP_K), jnp.int32), jax.ShapeDtypeStruct((1, N_EXPERTS), jnp.int32)],
        grid=(n // tm,),
        in_specs=[tok],
        out_specs=[tok, per_e],
        scratch_shapes=[pltpu.VMEM((1, N_EXPERTS), F32)],
        compiler_params=_cparams("arbitrary"),
        name="route_rank",
    )(top_i)
    counts = counts[0]
    padded = ((counts + tile - 1) // tile) * tile
    ends = jnp.cumsum(padded)
    n_tiles = (n * TOP_K) // tile + N_EXPERTS
    dump = n_tiles * tile
    tile_start = jnp.arange(n_tiles + 1, dtype=jnp.int32) * tile
    tile_expert = jnp.minimum(jnp.sum((tile_start[:, None] >= ends[None, :]).astype(jnp.int32), axis=1), N_EXPERTS - 1)
    n_used = (ends[-1:] // tile).astype(jnp.int32)
    last_tile = jnp.where(padded > 0, ends - tile, dump).astype(jnp.int32)
    pos = pl.pallas_call(
        _route_pos_kernel,
        out_shape=jax.ShapeDtypeStruct((n, TOP_K), jnp.int32),
        grid=(n // tm,),
        in_specs=[tok, tok, per_e],
        out_specs=tok,
        compiler_params=_cparams("arbitrary"),
        name="route_pos",
    )(top_i, rank, (ends - padded).astype(F32).reshape(1, N_EXPERTS))
    return pos, tile_expert.astype(jnp.int32), n_used, last_tile


def _token_copy(src_ref, src_idx, dst_ref, dst_idx, sem):
    return pltpu.make_async_copy(src_ref.at[_token_rows(src_idx), :], dst_ref.at[_token_rows(dst_idx), :], sem)


def _dispatch_kernel(pos_ref, last_ref, nu_ref, h_ref, xs_ref, zero_scr, sem, *, tile):
    n_tok = h_ref.shape[0] // SUBLANE

    @pl.when(pl.program_id(0) == 0)
    def _():
        zero_scr[...] = jnp.zeros_like(zero_scr)

        def fill(row):
            rows = pl.ds(pl.multiple_of(row * SUBLANE, tile * SUBLANE), tile * SUBLANE)
            return pltpu.make_async_copy(zero_scr, xs_ref.at[rows, :], sem)

        fills = [fill(last_ref[e]) for e in range(N_EXPERTS)]
        for c in fills:
            c.start()
        for c in fills:
            c.wait()

        def fill_unused(j, carry):
            c = fill(j * tile)
            c.start()
            c.wait()
            return carry

        lax.fori_loop(nu_ref[0], xs_ref.shape[0] // (tile * SUBLANE), fill_unused, 0)

    def issue(t, carry):
        for k in range(TOP_K):
            _token_copy(h_ref, t, xs_ref, pos_ref[t * TOP_K + k], sem).start(priority=k % 2)
        return carry

    def drain(t, carry):
        for k in range(TOP_K):
            _token_copy(h_ref, 0, xs_ref, 0, sem).wait()
        return carry

    lax.fori_loop(0, n_tok, issue, 0, unroll=2)
    lax.fori_loop(0, n_tok, drain, 0, unroll=2)


def dispatch(h, pos_flat, last_tile, n_used, tile):
    n = h.shape[0] // SUBLANE
    tm = 256
    n_slots = n * TOP_K + (N_EXPERTS + 1) * tile
    return pl.pallas_call(
        functools.partial(_dispatch_kernel, tile=tile),
        out_shape=jax.ShapeDtypeStruct((n_slots * SUBLANE, LANE), F32),
        grid=(n // tm,),
        in_specs=[
            pl.BlockSpec((tm * TOP_K,), lambda i: (i,), memory_space=pltpu.SMEM),
            pl.BlockSpec(memory_space=pltpu.SMEM),
            pl.BlockSpec(memory_space=pltpu.SMEM),
            _token_tiles(tm, lambda i: i),
        ],
        out_specs=pl.BlockSpec(memory_space=pl.ANY),
        scratch_shapes=[pltpu.VMEM((tile * SUBLANE, LANE), F32), pltpu.SemaphoreType.DMA(())],
        compiler_params=_cparams("arbitrary"),
        name="moe_dispatch",
    )(pos_flat, last_tile, n_used, h)


def _combine_kernel(pos_ref, pos_next_ref, x1_ref, tp_ref, g2_ref, fw_ref, ys_ref, o_ref, buf, sems,
                    *, n_tiles, mod_base, tiles_per_seq, final):
    i = pl.program_id(0)
    slot = i % 2
    tm = x1_ref.shape[0]

    def gather(p_ref, s):
        def issue(t, carry):
            for k in range(TOP_K):
                _token_copy(ys_ref, p_ref[t * TOP_K + k], buf.at[s, k], t, sems.at[s]).start(priority=k % 2)
            return carry
        lax.fori_loop(0, tm, issue, 0, unroll=2)

    @pl.when(i == 0)
    def _():
        gather(pos_ref, 0)

    @pl.when(i + 1 < n_tiles)
    def _():
        gather(pos_next_ref, 1 - slot)

    def drain(t, carry):
        for k in range(TOP_K):
            _token_copy(ys_ref, 0, buf.at[slot, 0], 0, sems.at[slot]).wait()
        return carry

    lax.fori_loop(0, tm, drain, 0, unroll=2)

    tp = tp_ref[...]
    acc = tp[:, 0:1] * _load_token_tiles(buf, tm, (slot, 0))
    for k in range(1, TOP_K):
        acc = acc + tp[:, k:k + 1] * _load_token_tiles(buf, tm, (slot, k))
    r = _mod_row(i, mod_base, tiles_per_seq)
    x2 = x1_ref[...] + g2_ref[pl.ds(r, 1), :] * acc
    o_ref[...] = _rms(x2, fw_ref[...]) if final else x2


def combine(x1, top_p, pos_flat, row0, ys, mod, final_w, layer, *, seq_len, mod_base, final):
    n, d = x1.shape
    tm = 256
    assert row0 % tm == 0
    r0 = row0 // tm
    n_tiles = n // tm
    kern = functools.partial(_combine_kernel, n_tiles=n_tiles, mod_base=mod_base, tiles_per_seq=seq_len // tm,
                             final=final)
    return pl.pallas_call(
        kern,
        out_shape=jax.ShapeDtypeStruct((n, d), F32),
        grid=(n_tiles,),
        in_specs=[
            pl.BlockSpec((tm * TOP_K,), lambda i: (r0 + i,), memory_space=pltpu.SMEM),
            pl.BlockSpec((tm * TOP_K,), lambda i: (r0 + jnp.minimum(i + 1, n_tiles - 1),), memory_space=pltpu.SMEM),
            pl.BlockSpec((tm, d), lambda i: (i, 0)),
            pl.BlockSpec((tm, TOP_K), lambda i: (i, 0)),
            pl.BlockSpec((None, MOD_ROWS, d), lambda i: (layer, 0, 5)),
            pl.BlockSpec((1, d), lambda i: (0, 0)),
            pl.BlockSpec(memory_space=pl.ANY),
        ],
        out_specs=pl.BlockSpec((tm, d), lambda i: (i, 0)),
        scratch_shapes=[pltpu.VMEM((2, TOP_K, tm * SUBLANE, LANE), F32), pltpu.SemaphoreType.DMA((2,))],
        compiler_params=_cparams("arbitrary"),
        name="moe_combine",
    )(pos_flat, pos_flat, x1, top_p, mod, final_w, ys)


MOE_TILE = 256


def _rope_tables(t):
    rows = t // GRID_W
    row = jnp.repeat(jnp.arange(rows, dtype=F32), GRID_W)
    col = jnp.tile(jnp.arange(GRID_W, dtype=F32), rows)
    n_freq = HEAD_DIM // 4
    inv = ROPE_THETA ** (-jnp.arange(n_freq, dtype=F32) / n_freq)
    ang = jnp.concatenate([row[:, None] * inv, col[:, None] * inv], axis=-1)
    cos = jnp.repeat(jnp.cos(ang), 2, axis=-1)
    sin = jnp.repeat(jnp.sin(ang), 2, axis=-1)
    even = (jnp.arange(HEAD_DIM) % 2 == 0)[None, :]
    return cos, jnp.where(even, -sin, 0.0), jnp.where(even, 0.0, sin)


def _layer_params(a, n_layers):
    d = D_MODEL
    w = MIX_W
    p = dict(a)
    for name in ("norm1_w", "norm2_w"):
        p[name] = a[name].reshape(n_layers, 1, d)
    for name in ("rwkv_a0", "rwkv_k_k", "rwkv_k_a", "rwkv_ln_w"):
        p[name] = a[name].reshape(n_layers, 1, w)
    p["rwkv_r_k"] = a["rwkv_r_k"].reshape(n_layers, 1, w)
    p["q_norm_w"] = a["q_norm_w"].reshape(n_layers, 1, HEAD_DIM)
    p["k_norm_w"] = a["k_norm_w"].reshape(n_layers, 1, HEAD_DIM)
    wr, ar, gr = a["rwkv_w_up"].shape[2], a["rwkv_a_up"].shape[1], a["rwkv_g_up"].shape[1]
    assert wr + ar + gr == RWKV_LORA
    p["w_up_pad"] = jnp.pad(a["rwkv_w_up"], ((0, 0), (0, 0), (0, ar + gr), (0, 0)))
    p["a_up_pad"] = jnp.pad(a["rwkv_a_up"], ((0, 0), (wr, gr), (0, 0)))
    p["g_up_pad"] = jnp.pad(a["rwkv_g_up"], ((0, 0), (wr + ar, 0), (0, 0)))
    head = jnp.arange(w, dtype=jnp.int32) // RWKV_N
    p["seg_ones"] = (head[:, None] == head[None, :]).astype(BF16)
    p["w_branch_bf"] = a["w_branch"].astype(BF16)
    p["w_out_bf"] = a["w_out"].astype(BF16)
    p["router_b"] = a["router_b"].reshape(n_layers, 1, N_EXPERTS)
    p["b_gate_up"] = a["b_gate_up"].reshape(n_layers, N_EXPERTS, 1, 2 * D_FF)
    p["b_down"] = a["b_down"].reshape(n_layers, N_EXPERTS, 1, d)
    return p


def _mixer(x, p, mod, layer, batch, seq_len, mod_base, latent_inputs):
    latent = latent_inputs is not None
    z = in_proj(x, p["norm1_w"], mod, p["w_in"], layer, 0, MAIN_COLS, seq_len=seq_len, mod_base=mod_base)
    zg = in_proj(x, p["norm1_w"], mod, p["w_in"], layer, MAIN_COLS, GATE_COLS, seq_len=seq_len, mod_base=mod_base)
    ops = rwkv_prep(z, p, layer, seq_len)
    if latent:
        rope, cache_k, cache_v, state_ret, state_rwkv = latent_inputs
        o_ret = retention(z, p["ret_decay"], layer, batch, seq_len, rope=rope, state0=state_ret)
        o_rwkv = rwkv_scan(ops, p["rwkv_ln_w"], layer, batch, seq_len, state0=state_rwkv)
        o_att = attention(z, p["q_norm_w"], p["k_norm_w"], layer, batch, seq_len,
                          rope=rope, cache_k=cache_k, cache_v=cache_v)
        ctx_out = None
    else:
        o_ret, s_ret = retention(z, p["ret_decay"], layer, batch, seq_len)
        o_rwkv, s_rwkv = rwkv_scan(ops, p["rwkv_ln_w"], layer, batch, seq_len)
        o_att, new_k, new_v = attention(z, p["q_norm_w"], p["k_norm_w"], layer, batch, seq_len)
        ctx_out = (new_k, new_v, s_ret, s_rwkv)
    merged = merge(x, o_ret, o_rwkv, o_att, zg, p, mod, layer, seq_len=seq_len, mod_base=mod_base)
    return merged, ctx_out


def kernel(x_prompt, x_sample, cache_k, cache_v, state_ret, state_rwkv, c, c_ctx, norm1_w, norm2_w, mod_w, mod_b, w_in, ret_decay, rwkv_mu, rwkv_w0, rwkv_w_up, rwkv_a0, rwkv_a_up, rwkv_g_up, rwkv_k_k, rwkv_k_a, rwkv_r_k, rwkv_ln_w, q_norm_w, k_norm_w, w_branch, w_out, router_w, router_b, w_gate_up, b_gate_up, w_down, b_down, final_norm_w):
    bp, tp, d = x_prompt.shape
    bs, ts, _ = x_sample.shape
    n_layers = w_in.shape[0]
    assert bs + 1 <= MOD_ROWS
    p = _layer_params(dict(
        norm1_w=norm1_w, norm2_w=norm2_w, w_in=w_in, ret_decay=ret_decay, rwkv_mu=rwkv_mu, rwkv_w0=rwkv_w0,
        rwkv_w_up=rwkv_w_up, rwkv_a0=rwkv_a0, rwkv_a_up=rwkv_a_up, rwkv_g_up=rwkv_g_up, rwkv_k_k=rwkv_k_k,
        rwkv_k_a=rwkv_k_a, rwkv_r_k=rwkv_r_k, rwkv_ln_w=rwkv_ln_w, q_norm_w=q_norm_w, k_norm_w=k_norm_w,
        w_branch=w_branch, w_out=w_out, router_w=router_w, router_b=router_b, w_gate_up=w_gate_up,
        b_gate_up=b_gate_up, w_down=w_down, b_down=b_down), n_layers)
    cond_rows = jnp.zeros((MOD_ROWS, d), F32).at[0].set(c_ctx).at[1:1 + bs].set(c)
    mod = modulation(cond_rows, mod_w, mod_b)
    rope = _rope_tables(ts)
    final_w = final_norm_w.reshape(1, d)
    np_, ns = bp * tp, bs * ts

    xp = x_prompt.reshape(np_, d)
    xs = x_sample.reshape(ns, d)
    ks_, vs_, srs, sws = [], [], [], []
    for layer in range(n_layers):
        (x1p, h2p, tip, tpp), (k_l, v_l, sr_l, sw_l) = _mixer(xp, p, mod, layer, bp, tp, 0, None)
        (x1s, h2s, tis, tps), _ = _mixer(xs, p, mod, layer, bs, ts, 1,
                                         (rope, cache_k, cache_v, state_ret, state_rwkv))
        ks_.append(k_l.reshape(bp, tp, ATT_KV_HEADS, HEAD_DIM))
        vs_.append(v_l.reshape(bp, tp, ATT_KV_HEADS, HEAD_DIM))
        srs.append(sr_l)
        sws.append(sw_l)
        h2 = jnp.concatenate([h2p, h2s], axis=0)
        pos, tile_expert, n_used, last_tile = route(jnp.concatenate([tip, tis], axis=0), MOE_TILE)
        pos_flat = pos.reshape(-1)
        xs_sorted = dispatch(h2, pos_flat, last_tile, n_used, MOE_TILE)
        ys = moe_experts(xs_sorted, tile_expert, n_used, p, layer, MOE_TILE)
        final = layer == n_layers - 1
        xp = combine(x1p, tpp, pos_flat, 0, ys, mod, final_w, layer, seq_len=tp, mod_base=0, final=final)
        xs = combine(x1s, tps, pos_flat, np_, ys, mod, final_w, layer, seq_len=ts, mod_base=1, final=final)
    return (xp.reshape(bp, tp, d), xs.reshape(bs, ts, d),
            jnp.stack(ks_, axis=1), jnp.stack(vs_, axis=1), jnp.stack(srs, axis=1), jnp.stack(sws, axis=1))
```

```python
import functools

import jax
import jax.numpy as jnp
from jax import lax
from jax.experimental import pallas as pl
from jax.experimental.pallas import tpu as pltpu

F32 = jnp.float32
BF16 = jnp.bfloat16
Z_DTYPE = BF16
Z_ROWS = 16

D_MODEL = 1024
DEPTH = 2
GRID_W = 64
MIX_W = D_MODEL // 2
N_BRANCH = 3
EPS = 1e-6
HEAD_DIM = 128
ATT_HEADS = MIX_W // HEAD_DIM
ATT_KV_HEADS = ATT_HEADS // 2
ROPE_THETA = 10000.0
RET_HEADS = MIX_W // HEAD_DIM
RET_CHUNK = 128
RET_UNROLL = 4
ATT_ROW_CHUNK = 128
RWKV_N = 64
RWKV_HEADS = MIX_W // RWKV_N
RWKV_CHUNK = 64
RWKV_GROUP = 4
RWKV_LORA = 256
RWKV_LN_EPS = 64e-5
N_EXPERTS = 32
TOP_K = 4
D_FF = D_MODEL
SWIGLU_LIMIT = 7.0
SWIGLU_ALPHA = 1.702
RET_COLS = 4 * MIX_W
RWKV_COLS = 3 * MIX_W + RWKV_LORA
ATT_COLS = (ATT_HEADS + 2 * ATT_KV_HEADS) * HEAD_DIM
GATE_COLS = N_BRANCH * D_MODEL
MAIN_COLS = RET_COLS + RWKV_COLS + ATT_COLS
N_MOD = 6
MOD_ROWS = 8

LANE = 128
SUBLANE = 8
VMEM_LIMIT = 56 * 1024 * 1024

_NT = (((1,), (1,)), ((), ()))
_TN = (((0,), (0,)), ((), ()))


def _cparams(*sem):
    return pltpu.CompilerParams(dimension_semantics=sem, vmem_limit_bytes=VMEM_LIMIT)


def _bdot(a, b):
    return jnp.dot(a.astype(BF16), b.astype(BF16), preferred_element_type=F32)


def _bdot_g(a, b, dims):
    return lax.dot_general(a.astype(BF16), b.astype(BF16), dims, preferred_element_type=F32)


def _split3(x):
    hi = x.astype(BF16)
    r1 = x - hi.astype(F32)
    mid = r1.astype(BF16)
    lo = (r1 - mid.astype(F32)).astype(BF16)
    return hi, mid, lo


def _dot_x3(a, b):
    ah = a.astype(BF16)
    bh = b.astype(BF16)
    al = (a - ah.astype(F32)).astype(BF16)
    bl = (b - bh.astype(F32)).astype(BF16)
    d = functools.partial(jnp.dot, preferred_element_type=F32)
    return d(ah, bh) + (d(ah, bl) + d(al, bh))


def _dot_exact_rhs(a, b_exact):
    hi, mid, lo = _split3(a)
    d = functools.partial(jnp.dot, preferred_element_type=F32)
    return d(hi, b_exact) + (d(mid, b_exact) + d(lo, b_exact))


def _sigmoid(x):
    return 1.0 / (1.0 + jnp.exp(-x))


def _silu(x):
    return x * _sigmoid(x)


def _mod_kernel(cond_ref, w_ref, b_ref, o_ref):
    s = _silu(cond_ref[...])
    o_ref[...] = _dot_x3(s, w_ref[...]) + b_ref[...]


def modulation(cond_rows, mod_w, mod_b):
    n_layers, d, cols = mod_w.shape
    tn = d
    return pl.pallas_call(
        _mod_kernel,
        out_shape=jax.ShapeDtypeStruct((n_layers, MOD_ROWS, cols), F32),
        grid=(n_layers, cols // tn),
        in_specs=[
            pl.BlockSpec((MOD_ROWS, d), lambda l, j: (0, 0)),
            pl.BlockSpec((None, d, tn), lambda l, j: (l, 0, j)),
            pl.BlockSpec((None, 1, tn), lambda l, j: (l, 0, j)),
        ],
        out_specs=pl.BlockSpec((None, MOD_ROWS, tn), lambda l, j: (l, 0, j)),
        compiler_params=_cparams("arbitrary", "arbitrary"),
        name="modulation",
    )(cond_rows, mod_w, mod_b.reshape(n_layers, 1, cols))


def _mod_row(i, mod_base, tiles_per_seq):
    if mod_base == 0:
        return 0
    return mod_base + i // tiles_per_seq


def _modnorm(x, nw, shift, scale):
    ms = jnp.mean(x * x, axis=-1, keepdims=True)
    return (x * lax.rsqrt(ms + EPS)) * nw * (1.0 + scale) + shift


def _in_proj_kernel(x_ref, nw_ref, sh_ref, sc_ref, w_ref, o_ref, h_scr, *, mod_base, chunk, chunks_per_seq):
    i = pl.program_id(0)
    chunks = x_ref.shape[0] // chunk

    @pl.when(pl.program_id(1) == 0)
    def _():
        def norm_chunk(c, carry):
            r = _mod_row(i * chunks + c, mod_base, chunks_per_seq)
            rows = pl.ds(pl.multiple_of(c * chunk, chunk), chunk)
            h = _modnorm(x_ref[rows, :], nw_ref[...], sh_ref[pl.ds(r, 1), :], sc_ref[pl.ds(r, 1), :])
            h_scr[rows, :] = h.astype(BF16)
            return carry

        lax.fori_loop(0, chunks, norm_chunk, 0)

    o_ref[...] = jnp.dot(h_scr[...], w_ref[...].astype(BF16), preferred_element_type=F32).astype(o_ref.dtype)


def in_proj(x, norm_w, mod, w_in, layer, col0, ncols, *, seq_len, mod_base):
    n, d = x.shape
    tm = min(n, 4096)
    tn = 256
    chunk = min(tm, 512)
    assert n % tm == 0 and ncols % tn == 0 and col0 % tn == 0 and (mod_base == 0 or seq_len % chunk == 0)
    j0 = col0 // tn
    kern = functools.partial(_in_proj_kernel, mod_base=mod_base, chunk=chunk, chunks_per_seq=seq_len // chunk)
    return pl.pallas_call(
        kern,
        out_shape=jax.ShapeDtypeStruct((n, ncols), Z_DTYPE),
        grid=(n // tm, ncols // tn),
        in_specs=[
            pl.BlockSpec((tm, d), lambda i, j: (i, 0)),
            pl.BlockSpec((None, 1, d), lambda i, j: (layer, 0, 0)),
            pl.BlockSpec((None, MOD_ROWS, d), lambda i, j: (layer, 0, 0)),
            pl.BlockSpec((None, MOD_ROWS, d), lambda i, j: (layer, 0, 1)),
            pl.BlockSpec((None, d, tn), lambda i, j: (layer, 0, j0 + j)),
        ],
        out_specs=pl.BlockSpec((tm, tn), lambda i, j: (i, j)),
        scratch_shapes=[pltpu.VMEM((tm, d), BF16)],
        compiler_params=_cparams("arbitrary", "arbitrary"),
        name="in_proj",
    )(x, norm_w, mod, mod, w_in)


def _rope(x, cos, sin_even, sin_odd):
    nxt = pltpu.roll(x, LANE - 1, axis=1)
    prv = pltpu.roll(x, 1, axis=1)
    return x * cos + nxt * sin_even + prv * sin_odd


def _rms(x, w):
    return x * lax.rsqrt(jnp.mean(x * x, axis=-1, keepdims=True) + EPS) * w


def _iota2(shape, dim):
    return lax.broadcasted_iota(jnp.int32, shape, dim)


def _retention_kernel(*refs, layer, latent, n_chunks):
    if latent:
        (decay_ref, q_ref, k_ref, v_ref, g_ref, cos_ref, se_ref, so_ref, s0_ref,
         o_ref, q_scr, k_scr, kf_scr, kb_scr) = refs
    else:
        (decay_ref, q_ref, k_ref, v_ref, g_ref,
         o_ref, sout_ref, q_scr, k_scr, kf_scr, kb_scr) = refs
    c_len = RET_CHUNK
    h = pl.program_id(1)
    lgf = -decay_ref[layer, 0, h]
    lgb = -decay_ref[layer, 1, h]

    if latent:
        q_scr[...] = _rope(q_ref[...].astype(F32), cos_ref[...], se_ref[...], so_ref[...])
        k_scr[...] = _rope(k_ref[...].astype(F32), cos_ref[...], se_ref[...], so_ref[...]) * (HEAD_DIM ** -0.5)
    else:
        q_scr[...] = q_ref[...].astype(F32)
        k_scr[...] = k_ref[...].astype(F32) * (HEAD_DIM ** -0.5)

    ii = _iota2((c_len, c_len), 0)
    jj = _iota2((c_len, c_len), 1)
    diff = (ii - jj).astype(F32)
    tok = ii.astype(F32)
    dcomb = (jnp.where(diff >= 0, jnp.exp(lgf * jnp.maximum(diff, 0.0)), 0.0)
             + jnp.where(diff <= 0, jnp.exp(lgb * jnp.maximum(-diff, 0.0)), 0.0))
    xi_f = jnp.exp(lgf * (tok + 1.0))
    xi_b = jnp.exp(lgb * (c_len - tok))
    zeta_f = jnp.exp(lgf * (c_len - 1.0 - tok))
    zeta_b = jnp.exp(lgb * tok)
    gc_f = jnp.exp(jnp.full((c_len, c_len), c_len, F32) * lgf)
    gc_b = jnp.exp(jnp.full((c_len, c_len), c_len, F32) * lgb)

    def rows(c):
        return pl.ds(pl.multiple_of(c * c_len, c_len), c_len)

    unroll = min(RET_UNROLL, n_chunks)
    assert n_chunks % unroll == 0

    def local(it, carry):
        cs = [it * unroll + u for u in range(unroll)]
        kcs = [k_scr[rows(c), :] for c in cs]
        vcs = [v_ref[rows(c), :] for c in cs]
        for c, kc, vc in zip(cs, kcs, vcs):
            kf_scr[c] = _bdot_g(kc * zeta_f, vc, _TN)
        for c, kc, vc in zip(cs, kcs, vcs):
            kb_scr[c] = _bdot_g(kc * zeta_b, vc, _TN)
        return carry

    lax.fori_loop(0, n_chunks // unroll, local, 0)

    if latent:
        sf0 = s0_ref[0]
        sb0 = s0_ref[1]
    else:
        sf0 = jnp.zeros((c_len, c_len), F32)
        sb0 = sf0

    def scan_f(c, s):
        kl = kf_scr[c]
        kf_scr[c] = s
        return s * gc_f + kl

    def scan_b(t, s):
        c = n_chunks - 1 - t
        kl = kb_scr[c]
        kb_scr[c] = s
        return s * gc_b + kl

    sf = lax.fori_loop(0, n_chunks, scan_f, sf0)
    sb = lax.fori_loop(0, n_chunks, scan_b, sb0)
    if not latent:
        sout_ref[0] = sf
        sout_ref[1] = sb

    def output(it, carry):
        cs = [it * unroll + u for u in range(unroll)]
        qcs = [q_scr[rows(c), :] for c in cs]
        atts = [_bdot_g(qc, k_scr[rows(c), :], _NT) * dcomb for c, qc in zip(cs, qcs)]
        inter = [_bdot(jnp.concatenate([qc * xi_f, qc * xi_b], axis=1),
                       jnp.concatenate([kf_scr[c], kb_scr[c]], axis=0)) for c, qc in zip(cs, qcs)]
        os_ = [_bdot(att, v_ref[rows(c), :]) + x for c, att, x in zip(cs, atts, inter)]
        mus = [jnp.mean(o, axis=-1, keepdims=True) for o in os_]
        ds = [o - mu for o, mu in zip(os_, mus)]
        vars_ = [jnp.mean(d * d, axis=-1, keepdims=True) for d in ds]
        for c, d, var in zip(cs, ds, vars_):
            o_ref[rows(c), :] = _silu(g_ref[rows(c), :].astype(F32)) * (d * lax.rsqrt(var + EPS))
        return carry

    lax.fori_loop(0, n_chunks // unroll, output, 0)


def retention(z, ret_decay, layer, batch, seq_len, *, rope=None, state0=None):
    latent = rope is not None
    n = z.shape[0]
    t = seq_len
    n_chunks = t // RET_CHUNK
    hd = HEAD_DIM
    nh = RET_HEADS
    col = lambda base: pl.BlockSpec((t, hd), lambda b, h, base=base: (b, base + h))
    in_specs = [pl.BlockSpec(memory_space=pltpu.SMEM), col(0), col(nh), col(2 * nh), col(3 * nh)]
    args = [ret_decay, z, z, z, z]
    out_shape = [jax.ShapeDtypeStruct((n, MIX_W), F32)]
    out_specs = [pl.BlockSpec((t, hd), lambda b, h: (b, h))]
    if latent:
        tab = pl.BlockSpec((t, hd), lambda b, h: (0, 0))
        in_specs += [tab, tab, tab,
                     pl.BlockSpec((None, None, 2, None, hd, hd), lambda b, h: (b, layer, 0, h, 0, 0))]
        args += [*rope, state0]
    else:
        out_shape.append(jax.ShapeDtypeStruct((batch, 2, nh, hd, hd), F32))
        out_specs.append(pl.BlockSpec((None, 2, None, hd, hd), lambda b, h: (b, 0, h, 0, 0)))
    kern = functools.partial(_retention_kernel, layer=layer, latent=latent, n_chunks=n_chunks)
    res = pl.pallas_call(
        kern,
        out_shape=out_shape,
        grid=(batch, nh),
        in_specs=in_specs,
        out_specs=out_specs,
        scratch_shapes=[pltpu.VMEM((t, hd), F32), pltpu.VMEM((t, hd), F32),
                        pltpu.VMEM((n_chunks, hd, hd), F32), pltpu.VMEM((n_chunks, hd, hd), F32)],
        compiler_params=_cparams("arbitrary", "arbitrary"),
        name="retention",
    )(*args)
    return res if not latent else res[0]


def _attention_kernel(*refs, latent, tq, past):
    if latent:
        (q_ref, k_ref, v_ref, qn_ref, kn_ref, cos_ref, se_ref, so_ref, ck_ref, cv_ref,
         o_ref, kb_scr, vb_scr) = refs
    else:
        (q_ref, k_ref, v_ref, qn_ref, kn_ref,
         o_ref, nk_ref, nv_ref, kb_scr, vb_scr) = refs
    qi = pl.program_id(2)
    hd = HEAD_DIM

    @pl.when(qi == 0)
    def _():
        kn = _rms(k_ref[...].astype(F32), kn_ref[...])
        if latent:
            kb_scr[0:past, :] = ck_ref[...].astype(BF16)
            vb_scr[0:past, :] = cv_ref[...].astype(BF16)
            kb_scr[past:, :] = _rope(kn, cos_ref[...], se_ref[...], so_ref[...]).astype(BF16)
            vb_scr[past:, :] = v_ref[...].astype(BF16)
        else:
            kb_scr[...] = kn.astype(BF16)
            vb_scr[...] = v_ref[...].astype(BF16)
            nk_ref[...] = kn
            nv_ref[...] = v_ref[...].astype(F32)

    q = q_ref[...].astype(F32)
    heads = []
    for g in range(2):
        qg = _rms(q[:, g * hd:(g + 1) * hd], qn_ref[...])
        if latent:
            r = pl.ds(pl.multiple_of(qi * tq, tq), tq)
            qg = _rope(qg, cos_ref[r, :], se_ref[r, :], so_ref[r, :])
        heads.append(qg)
    qq = jnp.concatenate(heads, axis=0).astype(BF16)
    rc = ATT_ROW_CHUNK
    n_rc = (2 * tq) // rc
    scale = hd ** -0.5
    scores, probs, outs = {}, {}, {}

    def score(c):
        scores[c] = lax.dot_general(qq[c * rc:(c + 1) * rc], kb_scr[...], _NT, preferred_element_type=F32)

    def soft(c):
        s = scores.pop(c)
        p = jnp.exp((s - jnp.max(s, axis=-1, keepdims=True)) * scale)
        probs[c] = (p.astype(BF16), jnp.sum(p, axis=-1, keepdims=True))

    def out(c):
        p, l = probs.pop(c)
        outs[c] = jnp.dot(p, vb_scr[...], preferred_element_type=F32) / l

    score(0)
    for c in range(n_rc):
        if c + 1 < n_rc:
            score(c + 1)
        soft(c)
        if c >= 1:
            out(c - 1)
    out(n_rc - 1)
    o = jnp.concatenate([outs[c] for c in range(n_rc)], axis=0)
    o_ref[...] = jnp.concatenate([o[:tq], o[tq:]], axis=1)


def attention(z, q_norm_w, k_norm_w, layer, batch, seq_len, *, rope=None, cache_k=None, cache_v=None):
    latent = rope is not None
    n = z.shape[0]
    t = seq_len
    hd = HEAD_DIM
    tq = min(t, 256)
    past = cache_k.shape[2] if latent else 0
    q0 = (RET_COLS + RWKV_COLS) // (2 * hd)
    k0 = (RET_COLS + RWKV_COLS) // hd + ATT_HEADS
    v0 = k0 + ATT_KV_HEADS
    nq = t // tq
    in_specs = [
        pl.BlockSpec((tq, 2 * hd), lambda b, g, i: (b * nq + i, q0 + g)),
        pl.BlockSpec((t, hd), lambda b, g, i: (b, k0 + g)),
        pl.BlockSpec((t, hd), lambda b, g, i: (b, v0 + g)),
        pl.BlockSpec((None, 1, hd), lambda b, g, i: (layer, 0, 0)),
        pl.BlockSpec((None, 1, hd), lambda b, g, i: (layer, 0, 0)),
    ]
    args = [z, z, z, q_norm_w, k_norm_w]
    out_shape = [jax.ShapeDtypeStruct((n, MIX_W), F32)]
    out_specs = [pl.BlockSpec((tq, 2 * hd), lambda b, g, i: (b * nq + i, g))]
    if latent:
        tab = pl.BlockSpec((t, hd), lambda b, g, i: (0, 0))
        cache = pl.BlockSpec((None, None, past, hd), lambda b, g, i: (b, layer, 0, g))
        in_specs += [tab, tab, tab, cache, cache]
        flat = cache_k.shape[:3] + (ATT_KV_HEADS * hd,)
        args += [*rope, cache_k.reshape(flat), cache_v.reshape(flat)]
    else:
        new = jax.ShapeDtypeStruct((batch, t, ATT_KV_HEADS * hd), F32)
        out_shape += [new, new]
        spec = pl.BlockSpec((None, t, hd), lambda b, g, i: (b, 0, g))
        out_specs += [spec, spec]
    kern = functools.partial(_attention_kernel, latent=latent, tq=tq, past=past)
    res = pl.pallas_call(
        kern,
        out_shape=out_shape,
        grid=(batch, ATT_KV_HEADS, nq),
        in_specs=in_specs,
        out_specs=out_specs,
        scratch_shapes=[pltpu.VMEM((past + t, hd), BF16), pltpu.VMEM((past + t, hd), BF16)],
        compiler_params=_cparams("arbitrary", "arbitrary", "arbitrary"),
        name="attention",
    )(*args)
    return res if not latent else res[0]


def _softplus(x):
    return jnp.maximum(x, 0.0) + jnp.log1p(jnp.exp(-jnp.abs(x)))


def _token_shift(u, prev_row, next_row, mu):
    tr = u.shape[0]
    row = _iota2(u.shape, 0)
    prev = jnp.where(row == 0, prev_row, pltpu.roll(u, 1, axis=0))
    nxt = jnp.where(row == tr - 1, next_row, pltpu.roll(u, tr - 1, axis=0))
    return u + mu[0:1, :] * (prev - u) + mu[1:2, :] * (nxt - u)


def _rwkv_prep_kernel(zr, zk, zv, zl, pr, pk, pv, pl_, nr, nk, nv, nl, mu_r, mu_k, mu_v, mu_l,
                      w0_ref, wup_ref, a0_ref, aup_ref, gup_ref, kkw_ref, kaw_ref, rkw_ref, seg_ref,
                      r_o, k_o, v_o, kk_o, b_o, lw0_o, lw1_o, bonus_o, gate_o, *, tiles_per_seq):
    i = pl.program_id(0)
    pos = i % tiles_per_seq
    keep_prev = jnp.where(pos == 0, 0.0, 1.0)
    keep_next = jnp.where(pos == tiles_per_seq - 1, 0.0, 1.0)

    def shifted(cur, prv, nxt, mu):
        return _token_shift(cur[...].astype(F32), prv[Z_ROWS - 1:Z_ROWS, :].astype(F32) * keep_prev,
                            nxt[0:1, :].astype(F32) * keep_next, mu[...])

    wr = shifted(zr, pr, nr, mu_r)
    wk = shifted(zk, pk, nk, mu_k)
    wv = shifted(zv, pv, nv, mu_v)
    lo = shifted(zl, pl_, nl, mu_l)

    seg = seg_ref[...]
    icl = _sigmoid(a0_ref[...] + _dot_x3(lo, aup_ref[...]))
    gate_o[...] = _dot_x3(_sigmoid(lo), gup_ref[...])
    th = jnp.tanh(lo)
    for d, lw_o in enumerate((lw0_o, lw1_o)):
        wl = w0_ref[d:d + 1, :] + _dot_x3(th, wup_ref[d])
        lw_o[...] = -jnp.exp(-_softplus(-wl) - 0.5)
    kk = wk * kkw_ref[...]
    kk = kk * lax.rsqrt(_dot_exact_rhs(kk * kk, seg) + 1e-12)
    k = wk * (1.0 + (icl - 1.0) * kaw_ref[...])
    r_o[...] = wr
    k_o[...] = k
    v_o[...] = wv
    kk_o[...] = kk
    b_o[...] = kk * icl
    bonus_o[...] = _dot_exact_rhs(wr * k * rkw_ref[...], seg) * wv


def rwkv_prep(z, p, layer, seq_len):
    n = z.shape[0]
    tr = 256
    w = MIX_W
    c0 = RET_COLS // w
    cl = (RET_COLS + 3 * w) // RWKV_LORA
    nt = n // tr
    rb = tr // Z_ROWS
    last = n // Z_ROWS - 1
    cur = lambda width, c: pl.BlockSpec((tr, width), lambda i, c=c: (i, c))
    prv = lambda width, c: pl.BlockSpec((Z_ROWS, width), lambda i, c=c: (jnp.maximum(i * rb - 1, 0), c))
    nxt = lambda width, c: pl.BlockSpec((Z_ROWS, width), lambda i, c=c: (jnp.minimum((i + 1) * rb, last), c))
    cols = [(w, c0), (w, c0 + 1), (w, c0 + 2), (RWKV_LORA, cl)]
    mu_cols = [(w, 0), (w, 1), (w, 2), (RWKV_LORA, 3 * w // RWKV_LORA)]
    whole = lambda *shape: pl.BlockSpec((None,) + shape, lambda i: (layer,) + (0,) * len(shape))
    in_specs = ([cur(*c) for c in cols] + [prv(*c) for c in cols] + [nxt(*c) for c in cols]
                + [pl.BlockSpec((None, 2, width), lambda i, c=c: (layer, 0, c)) for width, c in mu_cols]
                + [whole(2, w), whole(2, RWKV_LORA, w), whole(1, w), whole(RWKV_LORA, w), whole(RWKV_LORA, w),
                   whole(1, w), whole(1, w), whole(1, w), pl.BlockSpec((w, w), lambda i: (0, 0))])
    out = jax.ShapeDtypeStruct((n, w), F32)
    kern = functools.partial(_rwkv_prep_kernel, tiles_per_seq=seq_len // tr)
    return pl.pallas_call(
        kern,
        out_shape=[out] * 9,
        grid=(nt,),
        in_specs=in_specs,
        out_specs=[pl.BlockSpec((tr, w), lambda i: (i, 0))] * 9,
        compiler_params=_cparams("arbitrary"),
        name="rwkv_prep",
    )(*([z] * 12), *([p["rwkv_mu"]] * 4), p["rwkv_w0"], p["w_up_pad"], p["rwkv_a0"], p["a_up_pad"], p["g_up_pad"],
      p["rwkv_k_k"], p["rwkv_k_a"], p["rwkv_r_k"], p["seg_ones"])


def _tri_inverse_steps(mats, eye, blk16, off32, off64):
    ds = [jnp.where(blk16, a, 0.0) for a in mats]
    ts = [eye + d for d in ds]
    pws = ds
    for _ in range(3):
        pws = [_bdot(pw, pw) for pw in pws]
        yield None
        ts = [_bdot(t, eye + pw) for t, pw in zip(ts, pws)]
        yield None
    for off in (off32, off64):
        halves = [_bdot(t, jnp.where(off, a, 0.0)) for t, a in zip(ts, mats)]
        yield None
        ts = [t + _bdot(h, t) for t, h in zip(ts, halves)]
        yield None
    yield ts


def _rwkv_scan_kernel(*refs, latent, n_chunks, group, pairs):
    scratch = refs[-10:]
    yf_scr, yb_scr, s_scr, ar_s, tinv_s, av_s, mrbk_s, vst_s, bko_s, gt_s = scratch
    if latent:
        (r_ref, k_ref, v_ref, kk_ref, b_ref, lw0_ref, lw1_ref, bonus_ref, gate_ref, lnw_ref, s0_ref,
         o_ref) = refs[:-10]
    else:
        (r_ref, k_ref, v_ref, kk_ref, b_ref, lw0_ref, lw1_ref, bonus_ref, gate_ref, lnw_ref,
         o_ref, sout_ref) = refs[:-10]
    cl = RWKV_CHUNK
    hn = RWKV_N
    two = 2 * cl
    m0 = _iota2((1, LANE), 1) < hn
    row = _iota2((two, two), 0)
    col = _iota2((two, two), 1)
    same = (row >= cl) == (col >= cl)
    eye = jnp.where(row == col, 1.0, 0.0)
    blk16 = (row >> 4) == (col >> 4)
    off32 = ((row >> 5) == (col >> 5)) & ~blk16
    off64 = same & ((row >> 5) != (col >> 5))
    r64 = _iota2((cl, cl), 0)
    c64 = _iota2((cl, cl), 1)
    masks = (
        (same & (row >= col), same & (row > col), jnp.where(r64 >= c64, 1.0, 0.0).astype(BF16)),
        (same & (row <= col), same & (row < col), jnp.where(r64 <= c64, 1.0, 0.0).astype(BF16)),
    )

    def stack(x):
        return jnp.concatenate([jnp.where(m0, x, 0.0), jnp.where(m0, 0.0, x)], axis=0)

    def cols(p):
        return slice(p * LANE, (p + 1) * LANE)

    zero = jnp.zeros((hn, hn), F32)
    for d in range(2):
        for p in range(pairs):
            if latent:
                s_scr[d * pairs + p] = jnp.concatenate(
                    [jnp.concatenate([s0_ref[d, 2 * p], zero], axis=1),
                     jnp.concatenate([zero, s0_ref[d, 2 * p + 1]], axis=1)], axis=0)
            else:
                s_scr[d * pairs + p] = jnp.zeros((LANE, LANE), F32)

    lw_refs = (lw0_ref, lw1_ref)
    y_scrs = (yf_scr, yb_scr)

    def chunk_rows(g, j, d):
        ci = g * group + j
        ci = ci if d == 0 else n_chunks - 1 - ci
        return pl.ds(pl.multiple_of(ci * cl, cl), cl)

    units = [(j, d, p) for j in range(group) for d in range(2) for p in range(pairs)]
    dot = functools.partial(jnp.dot, preferred_element_type=F32)

    def prepare_steps(g, slot):
        rows = [(chunk_rows(g, j, d), cols(p)) for j, d, p in units]
        lws = [lw_refs[d][r, c] for (j, d, p), (r, c) in zip(units, rows)]
        parts = [_split3(lw) for lw in lws]
        lgs = [dot(masks[d][2], hi) + (dot(masks[d][2], mid) + dot(masks[d][2], lo))
               for (j, d, p), (hi, mid, lo) in zip(units, parts)]
        yield
        tots = [lg[cl - 1:cl, :] if d == 0 else lg[0:1, :] for (j, d, p), lg in zip(units, lgs)]
        ars, bks = [], []
        for i, (r, c) in enumerate(rows):
            lg, lw, tot = lgs[i], lws[i], tots[i]
            e_neg = jnp.exp(-lg)
            e_out = jnp.exp(tot - lg)
            bb = b_ref[r, c]
            kc = k_ref[r, c]
            ar = jnp.concatenate([stack(-kk_ref[r, c] * jnp.exp(lg - lw)), stack(r_ref[r, c] * jnp.exp(lg))], axis=0)
            ars.append(ar)
            bks.append(jnp.concatenate([stack(bb * e_neg), stack(kc * e_neg)], axis=0))
            ar_s[slot, i] = ar.astype(BF16)
            bko_s[slot, i] = jnp.concatenate([stack(bb * e_out), stack(kc * e_out)], axis=0).astype(BF16)
            gt_s[slot, i] = jnp.broadcast_to(jnp.exp(tot), (SUBLANE, LANE))
        yield
        crosses = [_bdot_g(ar, bk, _NT) for ar, bk in zip(ars, bks)]
        yield
        a_abs, a_aks = [], []
        for i, ((j, d, p), cross) in enumerate(zip(units, crosses)):
            incl, strict, _ = masks[d]
            a_abs.append(jnp.where(strict, cross[:two, :two], 0.0))
            a_aks.append(jnp.where(strict, cross[:two, two:], 0.0))
            mrbk_s[slot, i] = jnp.concatenate([jnp.where(incl, cross[two:, :two], 0.0),
                                               jnp.where(incl, cross[two:, two:], 0.0)], axis=1).astype(BF16)
        v_sts = [stack(v_ref[r, c]) for r, c in rows]
        for i, (a_ak, v_st) in enumerate(zip(a_aks, v_sts)):
            vst_s[slot, i] = v_st.astype(BF16)
            av_s[slot, i] = _bdot(a_ak, v_st)
        yield
        inverses = None
        for inverses in _tri_inverse_steps(a_abs, eye, blk16, off32, off64):
            yield
        for i, t_inv in enumerate(inverses):
            tinv_s[slot, i] = t_inv.astype(BF16)

    def advance_steps(g, slot):
        for j in range(group):
            chains = [(i, d, p) for i, (uj, d, p) in enumerate(units) if uj == j]
            ss = [s_scr[d * pairs + p] for i, d, p in chains]
            arss = [_bdot_g(ar_s[slot, i], s, _NT) for (i, d, p), s in zip(chains, ss)]
            us = [dot(tinv_s[slot, i], (ars[:two] + av_s[slot, i]).astype(BF16)) for (i, d, p), ars in zip(chains, arss)]
            uvs = [jnp.concatenate([u.astype(BF16), vst_s[slot, i]], axis=0) for (i, d, p), u in zip(chains, us)]
            for (i, d, p), s, ars, uv in zip(chains, ss, arss, uvs):
                y = ars[two:] + dot(mrbk_s[slot, i], uv)
                s_scr[d * pairs + p] = s * gt_s[slot, i][0:1, :] + lax.dot_general(uv, bko_s[slot, i], _TN,
                                                                                   preferred_element_type=F32)
                y_scrs[d][chunk_rows(g, j, d), cols(p)] = y[:cl] + y[cl:]
            yield

    def run(main, side=(), every=1):
        side = iter(side)
        for n_done, _ in enumerate(main, start=1):
            if n_done % every == 0:
                next(side, None)
        for _ in side:
            pass

    n_groups = n_chunks // group
    run(prepare_steps(0, 0))
    if n_groups > 1:
        def body(g, carry):
            run(prepare_steps(g, g % 2), advance_steps(g - 1, (g - 1) % 2), every=3)
            return carry

        lax.fori_loop(1, n_groups, body, 0)
    run(advance_steps(n_groups - 1, (n_groups - 1) % 2))

    if not latent:
        for d in range(2):
            for p in range(pairs):
                s = s_scr[d * pairs + p]
                sout_ref[d, 2 * p] = s[:hn, :hn]
                sout_ref[d, 2 * p + 1] = s[hn:, hn:]

    rt = 256 if (n_chunks * cl) % 256 == 0 else n_chunks * cl

    def epilogue(j, carry):
        rows = pl.ds(pl.multiple_of(j * rt, rt), rt)
        for p in range(pairs):
            c = cols(p)
            y = yf_scr[rows, c] + yb_scr[rows, c]
            s_all = jnp.sum(y, axis=-1, keepdims=True)
            s_0 = jnp.sum(jnp.where(m0, y, 0.0), axis=-1, keepdims=True)
            mu = jnp.where(m0, s_0, s_all - s_0) * (1.0 / hn)
            dlt = y - mu
            q_all = jnp.sum(dlt * dlt, axis=-1, keepdims=True)
            q_0 = jnp.sum(jnp.where(m0, dlt * dlt, 0.0), axis=-1, keepdims=True)
            var = jnp.where(m0, q_0, q_all - q_0) * (1.0 / hn)
            yn = dlt * lax.rsqrt(var + RWKV_LN_EPS) * lnw_ref[:, c] + bonus_ref[rows, c]
            o_ref[rows, c] = yn * gate_ref[rows, c]
        return carry

    lax.fori_loop(0, (n_chunks * cl) // rt, epilogue, 0)


def rwkv_scan(ops, ln_w, layer, batch, seq_len, *, state0=None):
    latent = state0 is not None
    n = ops[0].shape[0]
    t = seq_len
    hn = RWKV_N
    n_chunks = t // RWKV_CHUNK
    group = min(RWKV_GROUP, n_chunks)
    assert n_chunks % group == 0
    pairs = 2 if n_chunks == group else 1
    npair = RWKV_HEADS // 2
    assert npair % pairs == 0
    width = pairs * LANE
    blk = pl.BlockSpec((t, width), lambda b, h: (b, h))
    in_specs = [blk] * 9 + [pl.BlockSpec((None, 1, width), lambda b, h: (layer, 0, h))]
    args = [*ops, ln_w]
    out_shape = [jax.ShapeDtypeStruct((n, MIX_W), F32)]
    out_specs = [blk]
    if latent:
        in_specs.append(pl.BlockSpec((None, None, 2, 2 * pairs, hn, hn), lambda b, h: (b, layer, 0, h, 0, 0)))
        args.append(state0)
    else:
        out_shape.append(jax.ShapeDtypeStruct((batch, 2, RWKV_HEADS, hn, hn), F32))
        out_specs.append(pl.BlockSpec((None, 2, 2 * pairs, hn, hn), lambda b, h: (b, 0, h, 0, 0)))
    units = 2 * group * pairs
    two = 2 * RWKV_CHUNK
    kern = functools.partial(_rwkv_scan_kernel, latent=latent, n_chunks=n_chunks, group=group, pairs=pairs)
    res = pl.pallas_call(
        kern,
        out_shape=out_shape,
        grid=(batch, npair // pairs),
        in_specs=in_specs,
        out_specs=out_specs,
        scratch_shapes=[
            pltpu.VMEM((t, width), F32), pltpu.VMEM((t, width), F32), pltpu.VMEM((2 * pairs, LANE, LANE), F32),
            pltpu.VMEM((2, units, 2 * two, LANE), BF16),
            pltpu.VMEM((2, units, two, two), BF16),
            pltpu.VMEM((2, units, two, LANE), F32),
            pltpu.VMEM((2, units, two, 2 * two), BF16),
            pltpu.VMEM((2, units, two, LANE), BF16),
            pltpu.VMEM((2, units, 2 * two, LANE), BF16),
            pltpu.VMEM((2, units, SUBLANE, LANE), F32),
        ],
        compiler_params=_cparams("arbitrary", "arbitrary"),
        name="rwkv_scan",
    )(*args)
    return res if not latent else res[0]


assert D_MODEL == SUBLANE * LANE


def _token_tiles(n_tok, index_map):
    return pl.BlockSpec((n_tok * SUBLANE, LANE), lambda *a: (index_map(*a), 0))


def _token_rows(t):
    return pl.ds(pl.multiple_of(t * SUBLANE, SUBLANE), SUBLANE)


def _load_token_tiles(ref, n_tok, lead=()):
    return jnp.concatenate([ref[(*lead, pl.ds(j, n_tok, stride=SUBLANE), slice(None))] for j in range(SUBLANE)],
                           axis=1)


def _store_token_tiles(ref, x):
    for j in range(SUBLANE):
        ref[pl.ds(j, x.shape[0], stride=SUBLANE), :] = x[:, j * LANE:(j + 1) * LANE]


def _topk_softmax(logits):
    lane = _iota2(logits.shape, 1)
    out_lane = _iota2((logits.shape[0], TOP_K), 1)
    n_exp = logits.shape[1]
    vals, idxs = [], []
    work = logits
    for _ in range(TOP_K):
        m = jnp.max(work, axis=-1, keepdims=True)
        idx = jnp.min(jnp.where(work == m, lane, n_exp), axis=-1, keepdims=True)
        vals.append(m)
        idxs.append(idx)
        work = jnp.where(lane == idx, -jnp.inf, work)
    es = [jnp.exp(v - vals[0]) for v in vals]
    denom = es[0] + es[1] + es[2] + es[3]
    top_i = jnp.zeros((logits.shape[0], TOP_K), jnp.int32)
    top_p = jnp.zeros((logits.shape[0], TOP_K), F32)
    for j in range(TOP_K):
        top_i = jnp.where(out_lane == j, idxs[j], top_i)
        top_p = jnp.where(out_lane == j, es[j] / denom, top_p)
    return top_i, top_p


def _merge_kernel(x_ref, oret_ref, orwkv_ref, oatt_ref, zg_ref, wb_ref, wo_ref, g1_ref, nw2_ref, sh2_ref, sc2_ref,
                  rw_ref, rb_ref, x1_ref, h2_ref, ti_ref, tp_ref, *, mod_base, tiles_per_seq):
    r = _mod_row(pl.program_id(0), mod_base, tiles_per_seq)
    d = x_ref.shape[1]
    acc = None
    for nb, o_ref in enumerate((oret_ref, orwkv_ref, oatt_ref)):
        proj = jnp.dot(o_ref[...].astype(BF16), wb_ref[nb], preferred_element_type=F32)
        term = _sigmoid(zg_ref[:, nb * d:(nb + 1) * d].astype(F32)) * proj
        acc = term if acc is None else acc + term
    out = jnp.dot(acc.astype(BF16), wo_ref[...], preferred_element_type=F32)
    x1 = x_ref[...] + g1_ref[pl.ds(r, 1), :] * out
    x1_ref[...] = x1
    h2 = _modnorm(x1, nw2_ref[...], sh2_ref[pl.ds(r, 1), :], sc2_ref[pl.ds(r, 1), :])
    _store_token_tiles(h2_ref, h2)
    logits = _dot_x3(h2, rw_ref[...]) + rb_ref[...]
    top_i, top_p = _topk_softmax(logits)
    ti_ref[...] = top_i
    tp_ref[...] = top_p


def merge(x, o_ret, o_rwkv, o_att, zg, p, mod, layer, *, seq_len, mod_base):
    n, d = x.shape
    tm = 256
    w = MIX_W
    row = lambda width: pl.BlockSpec((tm, width), lambda i: (i, 0))
    modc = lambda c: pl.BlockSpec((None, MOD_ROWS, d), lambda i, c=c: (layer, 0, c))
    kern = functools.partial(_merge_kernel, mod_base=mod_base, tiles_per_seq=seq_len // tm)
    return pl.pallas_call(
        kern,
        out_shape=[jax.ShapeDtypeStruct((n, d), F32), jax.ShapeDtypeStruct((n * SUBLANE, LANE), F32),
                   jax.ShapeDtypeStruct((n, TOP_K), jnp.int32), jax.ShapeDtypeStruct((n, TOP_K), F32)],
        grid=(n // tm,),
        in_specs=[
            row(d), row(w), row(w), row(w), row(N_BRANCH * d),
            pl.BlockSpec((None, N_BRANCH, w, d), lambda i: (layer, 0, 0, 0)),
            pl.BlockSpec((None, d, d), lambda i: (layer, 0, 0)),
            modc(2),
            pl.BlockSpec((None, 1, d), lambda i: (layer, 0, 0)),
            modc(3), modc(4),
            pl.BlockSpec((None, d, N_EXPERTS), lambda i: (layer, 0, 0)),
            pl.BlockSpec((None, 1, N_EXPERTS), lambda i: (layer, 0, 0)),
        ],
        out_specs=[row(d), _token_tiles(tm, lambda i: i), row(TOP_K), row(TOP_K)],
        compiler_params=_cparams("arbitrary"),
        name="merge",
    )(x, o_ret, o_rwkv, o_att, zg, p["w_branch_bf"], p["w_out_bf"], mod, p["norm2_w"], mod, mod,
      p["router_w"], p["router_b"])


def _moe_kernel(te_ref, nu_ref, xs_ref, wgu_ref, bgu_ref, wdn_ref, bdn_ref, y_ref, wgu_scr, wdn_scr):
    i = pl.program_id(0)
    prev = te_ref[jnp.maximum(i - 1, 0)]

    @pl.when((i == 0) | (te_ref[i] != prev))
    def _():
        wgu_scr[...] = wgu_ref[...].astype(BF16)
        wdn_scr[...] = wdn_ref[...].astype(BF16)

    @pl.when(i >= nu_ref[0])
    def _():
        y_ref[...] = jnp.zeros_like(y_ref)

    @pl.when(i < nu_ref[0])
    def _():
        f = wdn_scr.shape[0]
        x = _load_token_tiles(xs_ref, xs_ref.shape[0] // SUBLANE).astype(BF16)
        gu = jnp.dot(x, wgu_scr[...], preferred_element_type=F32) + bgu_ref[...]
        gate = jnp.minimum(gu[:, :f], SWIGLU_LIMIT)
        up = jnp.clip(gu[:, f:], -SWIGLU_LIMIT, SWIGLU_LIMIT)
        act = (up + 1.0) * gate * _sigmoid(SWIGLU_ALPHA * gate)
        y = jnp.dot(act.astype(BF16), wdn_scr[...], preferred_element_type=F32) + bdn_ref[...]
        _store_token_tiles(y_ref, y)


def moe_experts(xs, tile_expert, n_used, p, layer, tile):
    s = xs.shape[0] // SUBLANE
    d = D_MODEL
    f = D_FF
    grid_spec = pltpu.PrefetchScalarGridSpec(
        num_scalar_prefetch=2,
        grid=(s // tile,),
        in_specs=[
            _token_tiles(tile, lambda i, te, nu: jnp.minimum(i, nu[0] - 1)),
            pl.BlockSpec((None, None, d, 2 * f), lambda i, te, nu: (layer, te[i], 0, 0)),
            pl.BlockSpec((None, None, 1, 2 * f), lambda i, te, nu: (layer, te[i], 0, 0)),
            pl.BlockSpec((None, None, f, d), lambda i, te, nu: (layer, te[i], 0, 0)),
            pl.BlockSpec((None, None, 1, d), lambda i, te, nu: (layer, te[i], 0, 0)),
        ],
        out_specs=_token_tiles(tile, lambda i, te, nu: i),
        scratch_shapes=[pltpu.VMEM((d, 2 * f), BF16), pltpu.VMEM((f, d), BF16)],
    )
    return pl.pallas_call(
        _moe_kernel,
        out_shape=jax.ShapeDtypeStruct((s * SUBLANE, LANE), F32),
        grid_spec=grid_spec,
        compiler_params=_cparams("arbitrary"),
        name="moe_experts",
    )(tile_expert, n_used, xs, p["w_gate_up"], p["b_gate_up"], p["w_down"], p["b_down"])


def _choice_onehots(top_i):
    lane = _iota2((top_i.shape[0], N_EXPERTS), 1)
    return [jnp.where(top_i[:, k:k + 1] == lane, 1.0, 0.0) for k in range(TOP_K)]


def _per_choice(onehots, table):
    out_lane = _iota2((onehots[0].shape[0], TOP_K), 1)
    out = jnp.zeros((onehots[0].shape[0], TOP_K), F32)
    for k, oh in enumerate(onehots):
        out = jnp.where(out_lane == k, jnp.sum(oh * table, axis=-1, keepdims=True), out)
    return out


def _route_rank_kernel(ti_ref, rank_ref, cnt_ref, run_scr):
    @pl.when(pl.program_id(0) == 0)
    def _():
        run_scr[...] = jnp.zeros_like(run_scr)

    tm = ti_ref.shape[0]
    ohs = _choice_onehots(ti_ref[...])
    chosen = ohs[0] + ohs[1] + ohs[2] + ohs[3]
    earlier = jnp.where(_iota2((tm, tm), 0) > _iota2((tm, tm), 1), 1.0, 0.0).astype(BF16)
    before = jnp.dot(earlier, chosen.astype(BF16), preferred_element_type=F32) + run_scr[...]
    rank_ref[...] = _per_choice(ohs, before).astype(jnp.int32)
    run_scr[...] += jnp.sum(chosen, axis=0, keepdims=True)
    cnt_ref[...] = run_scr[...].astype(jnp.int32)


def _route_pos_kernel(ti_ref, rank_ref, off_ref, pos_ref):
    ohs = _choice_onehots(ti_ref[...])
    pos_ref[...] = rank_ref[...] + _per_choice(ohs, off_ref[...]).astype(jnp.int32)


def route(top_i, tile):
    n = top_i.shape[0]
    tm = next(t for t in (1024, 768, 512, 256) if n % t == 0)
    tok = pl.BlockSpec((tm, TOP_K), lambda i: (i, 0))
    per_e = pl.BlockSpec((1, N_EXPERTS), lambda i: (0, 0))
    rank, counts = pl.pallas_call(
        _route_rank_kernel,
        out_shape=[jax.ShapeDtypeStruct((n, TOP_K), jnp.int32), jax.ShapeDtypeStruct((1, N_EXPERTS), jnp.int32)],
        grid=(n // tm,),
        in_specs=[tok],
        out_specs=[tok, per_e],
        scratch_shapes=[pltpu.VMEM((1, N_EXPERTS), F32)],
        compiler_params=_cparams("arbitrary"),
        name="route_rank",
    )(top_i)
    counts = counts[0]
    padded = ((counts + tile - 1) // tile) * tile
    ends = jnp.cumsum(padded)
    n_tiles = (n * TOP_K) // tile + N_EXPERTS
    dump = n_tiles * tile
    tile_start = jnp.arange(n_tiles + 1, dtype=jnp.int32) * tile
    tile_expert = jnp.minimum(jnp.sum((tile_start[:, None] >= ends[None, :]).astype(jnp.int32), axis=1), N_EXPERTS - 1)
    n_used = (ends[-1:] // tile).astype(jnp.int32)
    last_tile = jnp.where(padded > 0, ends - tile, dump).astype(jnp.int32)
    pos = pl.pallas_call(
        _route_pos_kernel,
        out_shape=jax.ShapeDtypeStruct((n, TOP_K), jnp.int32),
        grid=(n // tm,),
        in_specs=[tok, tok, per_e],
        out_specs=tok,
        compiler_params=_cparams("arbitrary"),
        name="route_pos",
    )(top_i, rank, (ends - padded).astype(F32).reshape(1, N_EXPERTS))
    return pos, tile_expert.astype(jnp.int32), n_used, last_tile


def _token_copy(src_ref, src_idx, dst_ref, dst_idx, sem):
    return pltpu.make_async_copy(src_ref.at[_token_rows(src_idx), :], dst_ref.at[_token_rows(dst_idx), :], sem)


def _dispatch_kernel(pos_ref, last_ref, nu_ref, h_ref, xs_ref, zero_scr, sem, *, tile):
    n_tok = h_ref.shape[0] // SUBLANE

    @pl.when(pl.program_id(0) == 0)
    def _():
        zero_scr[...] = jnp.zeros_like(zero_scr)

        def fill(row):
            rows = pl.ds(pl.multiple_of(row * SUBLANE, tile * SUBLANE), tile * SUBLANE)
            return pltpu.make_async_copy(zero_scr, xs_ref.at[rows, :], sem)

        fills = [fill(last_ref[e]) for e in range(N_EXPERTS)]
        for c in fills:
            c.start()
        for c in fills:
            c.wait()

        def fill_unused(j, carry):
            c = fill(j * tile)
            c.start()
            c.wait()
            return carry

        lax.fori_loop(nu_ref[0], xs_ref.shape[0] // (tile * SUBLANE), fill_unused, 0)

    def issue(t, carry):
        for k in range(TOP_K):
            _token_copy(h_ref, t, xs_ref, pos_ref[t * TOP_K + k], sem).start(priority=k % 2)
        return carry

    def drain(t, carry):
        for k in range(TOP_K):
            _token_copy(h_ref, 0, xs_ref, 0, sem).wait()
        return carry

    lax.fori_loop(0, n_tok, issue, 0, unroll=2)
    lax.fori_loop(0, n_tok, drain, 0, unroll=2)


def dispatch(h, pos_flat, last_tile, n_used, tile):
    n = h.shape[0] // SUBLANE
    tm = 256
    n_slots = n * TOP_K + (N_EXPERTS + 1) * tile
    return pl.pallas_call(
        functools.partial(_dispatch_kernel, tile=tile),
        out_shape=jax.ShapeDtypeStruct((n_slots * SUBLANE, LANE), F32),
        grid=(n // tm,),
        in_specs=[
            pl.BlockSpec((tm * TOP_K,), lambda i: (i,), memory_space=pltpu.SMEM),
            pl.BlockSpec(memory_space=pltpu.SMEM),
            pl.BlockSpec(memory_space=pltpu.SMEM),
            _token_tiles(tm, lambda i: i),
        ],
        out_specs=pl.BlockSpec(memory_space=pl.ANY),
        scratch_shapes=[pltpu.VMEM((tile * SUBLANE, LANE), F32), pltpu.SemaphoreType.DMA(())],
        compiler_params=_cparams("arbitrary"),
        name="moe_dispatch",
    )(pos_flat, last_tile, n_used, h)


def _combine_kernel(pos_ref, pos_next_ref, x1_ref, tp_ref, g2_ref, fw_ref, ys_ref, o_ref, buf, sems,
                    *, n_tiles, mod_base, tiles_per_seq, final):
    i = pl.program_id(0)
    slot = i % 2
    tm = x1_ref.shape[0]

    def gather(p_ref, s):
        def issue(t, carry):
            for k in range(TOP_K):
                _token_copy(ys_ref, p_ref[t * TOP_K + k], buf.at[s, k], t, sems.at[s]).start(priority=k % 2)
            return carry
        lax.fori_loop(0, tm, issue, 0, unroll=2)

    @pl.when(i == 0)
    def _():
        gather(pos_ref, 0)

    @pl.when(i + 1 < n_tiles)
    def _():
        gather(pos_next_ref, 1 - slot)

    def drain(t, carry):
        for k in range(TOP_K):
            _token_copy(ys_ref, 0, buf.at[slot, 0], 0, sems.at[slot]).wait()
        return carry

    lax.fori_loop(0, tm, drain, 0, unroll=2)

    tp = tp_ref[...]
    acc = tp[:, 0:1] * _load_token_tiles(buf, tm, (slot, 0))
    for k in range(1, TOP_K):
        acc = acc + tp[:, k:k + 1] * _load_token_tiles(buf, tm, (slot, k))
    r = _mod_row(i, mod_base, tiles_per_seq)
    x2 = x1_ref[...] + g2_ref[pl.ds(r, 1), :] * acc
    o_ref[...] = _rms(x2, fw_ref[...]) if final else x2


def combine(x1, top_p, pos_flat, row0, ys, mod, final_w, layer, *, seq_len, mod_base, final):
    n, d = x1.shape
    tm = 512
    assert row0 % tm == 0 and n % tm == 0 and (mod_base == 0 or seq_len % tm == 0)
    r0 = row0 // tm
    n_tiles = n // tm
    kern = functools.partial(_combine_kernel, n_tiles=n_tiles, mod_base=mod_base, tiles_per_seq=seq_len // tm,
                             final=final)
    return pl.pallas_call(
        kern,
        out_shape=jax.ShapeDtypeStruct((n, d), F32),
        grid=(n_tiles,),
        in_specs=[
            pl.BlockSpec((tm * TOP_K,), lambda i: (r0 + i,), memory_space=pltpu.SMEM),
            pl.BlockSpec((tm * TOP_K,), lambda i: (r0 + jnp.minimum(i + 1, n_tiles - 1),), memory_space=pltpu.SMEM),
            pl.BlockSpec((tm, d), lambda i: (i, 0)),
            pl.BlockSpec((tm, TOP_K), lambda i: (i, 0)),
            pl.BlockSpec((None, MOD_ROWS, d), lambda i: (layer, 0, 5)),
            pl.BlockSpec((1, d), lambda i: (0, 0)),
            pl.BlockSpec(memory_space=pl.ANY),
        ],
        out_specs=pl.BlockSpec((tm, d), lambda i: (i, 0)),
        scratch_shapes=[pltpu.VMEM((2, TOP_K, tm * SUBLANE, LANE), F32), pltpu.SemaphoreType.DMA((2,))],
        compiler_params=_cparams("arbitrary"),
        name="moe_combine",
    )(pos_flat, pos_flat, x1, top_p, mod, final_w, ys)


MOE_TILE = 256


def _rope_tables(t):
    rows = t // GRID_W
    row = jnp.repeat(jnp.arange(rows, dtype=F32), GRID_W)
    col = jnp.tile(jnp.arange(GRID_W, dtype=F32), rows)
    n_freq = HEAD_DIM // 4
    inv = ROPE_THETA ** (-jnp.arange(n_freq, dtype=F32) / n_freq)
    ang = jnp.concatenate([row[:, None] * inv, col[:, None] * inv], axis=-1)
    cos = jnp.repeat(jnp.cos(ang), 2, axis=-1)
    sin = jnp.repeat(jnp.sin(ang), 2, axis=-1)
    even = (jnp.arange(HEAD_DIM) % 2 == 0)[None, :]
    return cos, jnp.where(even, -sin, 0.0), jnp.where(even, 0.0, sin)


def _layer_params(a, n_layers):
    d = D_MODEL
    w = MIX_W
    p = dict(a)
    for name in ("norm1_w", "norm2_w"):
        p[name] = a[name].reshape(n_layers, 1, d)
    for name in ("rwkv_a0", "rwkv_k_k", "rwkv_k_a", "rwkv_ln_w"):
        p[name] = a[name].reshape(n_layers, 1, w)
    p["rwkv_r_k"] = a["rwkv_r_k"].reshape(n_layers, 1, w)
    p["q_norm_w"] = a["q_norm_w"].reshape(n_layers, 1, HEAD_DIM)
    p["k_norm_w"] = a["k_norm_w"].reshape(n_layers, 1, HEAD_DIM)
    wr, ar, gr = a["rwkv_w_up"].shape[2], a["rwkv_a_up"].shape[1], a["rwkv_g_up"].shape[1]
    assert wr + ar + gr == RWKV_LORA
    p["w_up_pad"] = jnp.pad(a["rwkv_w_up"], ((0, 0), (0, 0), (0, ar + gr), (0, 0)))
    p["a_up_pad"] = jnp.pad(a["rwkv_a_up"], ((0, 0), (wr, gr), (0, 0)))
    p["g_up_pad"] = jnp.pad(a["rwkv_g_up"], ((0, 0), (wr + ar, 0), (0, 0)))
    head = jnp.arange(w, dtype=jnp.int32) // RWKV_N
    p["seg_ones"] = (head[:, None] == head[None, :]).astype(BF16)
    p["w_branch_bf"] = a["w_branch"].astype(BF16)
    p["w_out_bf"] = a["w_out"].astype(BF16)
    p["router_b"] = a["router_b"].reshape(n_layers, 1, N_EXPERTS)
    p["b_gate_up"] = a["b_gate_up"].reshape(n_layers, N_EXPERTS, 1, 2 * D_FF)
    p["b_down"] = a["b_down"].reshape(n_layers, N_EXPERTS, 1, d)
    return p


def _mixer(x, p, mod, layer, batch, seq_len, mod_base, latent_inputs):
    latent = latent_inputs is not None
    z = in_proj(x, p["norm1_w"], mod, p["w_in"], layer, 0, MAIN_COLS, seq_len=seq_len, mod_base=mod_base)
    zg = in_proj(x, p["norm1_w"], mod, p["w_in"], layer, MAIN_COLS, GATE_COLS, seq_len=seq_len, mod_base=mod_base)
    ops = rwkv_prep(z, p, layer, seq_len)
    if latent:
        rope, cache_k, cache_v, state_ret, state_rwkv = latent_inputs
        o_ret = retention(z, p["ret_decay"], layer, batch, seq_len, rope=rope, state0=state_ret)
        o_rwkv = rwkv_scan(ops, p["rwkv_ln_w"], layer, batch, seq_len, state0=state_rwkv)
        o_att = attention(z, p["q_norm_w"], p["k_norm_w"], layer, batch, seq_len,
                          rope=rope, cache_k=cache_k, cache_v=cache_v)
        ctx_out = None
    else:
        o_ret, s_ret = retention(z, p["ret_decay"], layer, batch, seq_len)
        o_rwkv, s_rwkv = rwkv_scan(ops, p["rwkv_ln_w"], layer, batch, seq_len)
        o_att, new_k, new_v = attention(z, p["q_norm_w"], p["k_norm_w"], layer, batch, seq_len)
        ctx_out = (new_k, new_v, s_ret, s_rwkv)
    merged = merge(x, o_ret, o_rwkv, o_att, zg, p, mod, layer, seq_len=seq_len, mod_base=mod_base)
    return merged, ctx_out


def kernel(x_prompt, x_sample, cache_k, cache_v, state_ret, state_rwkv, c, c_ctx, norm1_w, norm2_w, mod_w, mod_b, w_in, ret_decay, rwkv_mu, rwkv_w0, rwkv_w_up, rwkv_a0, rwkv_a_up, rwkv_g_up, rwkv_k_k, rwkv_k_a, rwkv_r_k, rwkv_ln_w, q_norm_w, k_norm_w, w_branch, w_out, router_w, router_b, w_gate_up, b_gate_up, w_down, b_down, final_norm_w):
    bp, tp, d = x_prompt.shape
    bs, ts, _ = x_sample.shape
    n_layers = w_in.shape[0]
    assert bs + 1 <= MOD_ROWS
    p = _layer_params(dict(
        norm1_w=norm1_w, norm2_w=norm2_w, w_in=w_in, ret_decay=ret_decay, rwkv_mu=rwkv_mu, rwkv_w0=rwkv_w0,
        rwkv_w_up=rwkv_w_up, rwkv_a0=rwkv_a0, rwkv_a_up=rwkv_a_up, rwkv_g_up=rwkv_g_up, rwkv_k_k=rwkv_k_k,
        rwkv_k_a=rwkv_k_a, rwkv_r_k=rwkv_r_k, rwkv_ln_w=rwkv_ln_w, q_norm_w=q_norm_w, k_norm_w=k_norm_w,
        w_branch=w_branch, w_out=w_out, router_w=router_w, router_b=router_b, w_gate_up=w_gate_up,
        b_gate_up=b_gate_up, w_down=w_down, b_down=b_down), n_layers)
    cond_rows = jnp.zeros((MOD_ROWS, d), F32).at[0].set(c_ctx).at[1:1 + bs].set(c)
    mod = modulation(cond_rows, mod_w, mod_b)
    rope = _rope_tables(ts)
    final_w = final_norm_w.reshape(1, d)
    np_, ns = bp * tp, bs * ts

    xp = x_prompt.reshape(np_, d)
    xs = x_sample.reshape(ns, d)
    ks_, vs_, srs, sws = [], [], [], []
    for layer in range(n_layers):
        (x1p, h2p, tip, tpp), (k_l, v_l, sr_l, sw_l) = _mixer(xp, p, mod, layer, bp, tp, 0, None)
        (x1s, h2s, tis, tps), _ = _mixer(xs, p, mod, layer, bs, ts, 1,
                                         (rope, cache_k, cache_v, state_ret, state_rwkv))
        ks_.append(k_l.reshape(bp, tp, ATT_KV_HEADS, HEAD_DIM))
        vs_.append(v_l.reshape(bp, tp, ATT_KV_HEADS, HEAD_DIM))
        srs.append(sr_l)
        sws.append(sw_l)
        h2 = jnp.concatenate([h2p, h2s], axis=0)
        pos, tile_expert, n_used, last_tile = route(jnp.concatenate([tip, tis], axis=0), MOE_TILE)
        pos_flat = pos.reshape(-1)
        xs_sorted = dispatch(h2, pos_flat, last_tile, n_used, MOE_TILE)
        ys = moe_experts(xs_sorted, tile_expert, n_used, p, layer, MOE_TILE)
        final = layer == n_layers - 1
        xp = combine(x1p, tpp, pos_flat, 0, ys, mod, final_w, layer, seq_len=tp, mod_base=0, final=final)
        xs = combine(x1s, tps, pos_flat, np_, ys, mod, final_w, layer, seq_len=ts, mod_base=1, final=final)
    return (xp.reshape(bp, tp, d), xs.reshape(bs, ts, d),
            jnp.stack(ks_, axis=1), jnp.stack(vs_, axis=1), jnp.stack(srs, axis=1), jnp.stack(sws, axis=1))
```

```python
import functools

import jax
import jax.numpy as jnp
from jax import lax
from jax.experimental import pallas as pl
from jax.experimental.pallas import tpu as pltpu

F32 = jnp.float32
BF16 = jnp.bfloat16
Z_DTYPE = BF16
Z_ROWS = 16

D_MODEL = 1024
DEPTH = 2
GRID_W = 64
MIX_W = D_MODEL // 2
N_BRANCH = 3
EPS = 1e-6
HEAD_DIM = 128
ATT_HEADS = MIX_W // HEAD_DIM
ATT_KV_HEADS = ATT_HEADS // 2
ROPE_THETA = 10000.0
RET_HEADS = MIX_W // HEAD_DIM
RET_CHUNK = 128
RET_UNROLL = 4
ATT_ROW_CHUNK = 128
RWKV_N = 64
RWKV_HEADS = MIX_W // RWKV_N
RWKV_CHUNK = 64
RWKV_GROUP = 4
RWKV_LORA = 256
RWKV_LN_EPS = 64e-5
N_EXPERTS = 32
TOP_K = 4
D_FF = D_MODEL
SWIGLU_LIMIT = 7.0
SWIGLU_ALPHA = 1.702
RET_COLS = 4 * MIX_W
RWKV_COLS = 3 * MIX_W + RWKV_LORA
ATT_COLS = (ATT_HEADS + 2 * ATT_KV_HEADS) * HEAD_DIM
GATE_COLS = N_BRANCH * D_MODEL
MAIN_COLS = RET_COLS + RWKV_COLS + ATT_COLS
N_MOD = 6
MOD_ROWS = 8

LANE = 128
SUBLANE = 8
VMEM_LIMIT = 56 * 1024 * 1024

_NT = (((1,), (1,)), ((), ()))
_TN = (((0,), (0,)), ((), ()))


def _cparams(*sem):
    return pltpu.CompilerParams(dimension_semantics=sem, vmem_limit_bytes=VMEM_LIMIT)


def _bdot(a, b):
    return jnp.dot(a.astype(BF16), b.astype(BF16), preferred_element_type=F32)


def _bdot_g(a, b, dims):
    return lax.dot_general(a.astype(BF16), b.astype(BF16), dims, preferred_element_type=F32)


def _split2(x):
    hi = x.astype(BF16)
    lo = (x - hi.astype(F32)).astype(BF16)
    return hi, lo


def _dot_x3(a, b):
    ah = a.astype(BF16)
    bh = b.astype(BF16)
    al = (a - ah.astype(F32)).astype(BF16)
    bl = (b - bh.astype(F32)).astype(BF16)
    d = functools.partial(jnp.dot, preferred_element_type=F32)
    return d(ah, bh) + (d(ah, bl) + d(al, bh))


def _dot_exact_rhs(a, b_exact):
    hi, lo = _split2(a)
    d = functools.partial(jnp.dot, preferred_element_type=F32)
    return d(hi, b_exact) + d(lo, b_exact)


def _sigmoid(x):
    return 1.0 / (1.0 + jnp.exp(-x))


def _silu(x):
    return x * _sigmoid(x)


def _mod_kernel(cond_ref, w_ref, b_ref, o_ref):
    s = _silu(cond_ref[...])
    o_ref[...] = _dot_x3(s, w_ref[...]) + b_ref[...]


def modulation(cond_rows, mod_w, mod_b):
    n_layers, d, cols = mod_w.shape
    tn = d
    return pl.pallas_call(
        _mod_kernel,
        out_shape=jax.ShapeDtypeStruct((n_layers, MOD_ROWS, cols), F32),
        grid=(n_layers, cols // tn),
        in_specs=[
            pl.BlockSpec((MOD_ROWS, d), lambda l, j: (0, 0)),
            pl.BlockSpec((None, d, tn), lambda l, j: (l, 0, j)),
            pl.BlockSpec((None, 1, tn), lambda l, j: (l, 0, j)),
        ],
        out_specs=pl.BlockSpec((None, MOD_ROWS, tn), lambda l, j: (l, 0, j)),
        compiler_params=_cparams("arbitrary", "arbitrary"),
        name="modulation",
    )(cond_rows, mod_w, mod_b.reshape(n_layers, 1, cols))


def _mod_row(i, mod_base, tiles_per_seq):
    if mod_base == 0:
        return 0
    return mod_base + i // tiles_per_seq


def _modnorm(x, nw, shift, scale):
    ms = jnp.mean(x * x, axis=-1, keepdims=True)
    return (x * lax.rsqrt(ms + EPS)) * nw * (1.0 + scale) + shift


def _in_proj_kernel(x_ref, nw_ref, sh_ref, sc_ref, w_ref, o_ref, h_scr, *, mod_base, chunk, chunks_per_seq):
    i = pl.program_id(0)
    chunks = x_ref.shape[0] // chunk

    @pl.when(pl.program_id(1) == 0)
    def _():
        def norm_chunk(c, carry):
            r = _mod_row(i * chunks + c, mod_base, chunks_per_seq)
            rows = pl.ds(pl.multiple_of(c * chunk, chunk), chunk)
            h = _modnorm(x_ref[rows, :], nw_ref[...], sh_ref[pl.ds(r, 1), :], sc_ref[pl.ds(r, 1), :])
            h_scr[rows, :] = h.astype(BF16)
            return carry

        lax.fori_loop(0, chunks, norm_chunk, 0)

    o_ref[...] = jnp.dot(h_scr[...], w_ref[...].astype(BF16), preferred_element_type=F32).astype(o_ref.dtype)


def in_proj(x, norm_w, mod, w_in, layer, col0, ncols, *, seq_len, mod_base):
    n, d = x.shape
    tm = min(n, 4096)
    tn = 256
    chunk = min(tm, 512)
    assert n % tm == 0 and ncols % tn == 0 and col0 % tn == 0 and (mod_base == 0 or seq_len % chunk == 0)
    j0 = col0 // tn
    kern = functools.partial(_in_proj_kernel, mod_base=mod_base, chunk=chunk, chunks_per_seq=seq_len // chunk)
    return pl.pallas_call(
        kern,
        out_shape=jax.ShapeDtypeStruct((n, ncols), Z_DTYPE),
        grid=(n // tm, ncols // tn),
        in_specs=[
            pl.BlockSpec((tm, d), lambda i, j: (i, 0)),
            pl.BlockSpec((None, 1, d), lambda i, j: (layer, 0, 0)),
            pl.BlockSpec((None, MOD_ROWS, d), lambda i, j: (layer, 0, 0)),
            pl.BlockSpec((None, MOD_ROWS, d), lambda i, j: (layer, 0, 1)),
            pl.BlockSpec((None, d, tn), lambda i, j: (layer, 0, j0 + j)),
        ],
        out_specs=pl.BlockSpec((tm, tn), lambda i, j: (i, j)),
        scratch_shapes=[pltpu.VMEM((tm, d), BF16)],
        compiler_params=_cparams("arbitrary", "arbitrary"),
        name="in_proj",
    )(x, norm_w, mod, mod, w_in)


def _rope(x, cos, sin_even, sin_odd):
    nxt = pltpu.roll(x, LANE - 1, axis=1)
    prv = pltpu.roll(x, 1, axis=1)
    return x * cos + nxt * sin_even + prv * sin_odd


def _rms(x, w):
    return x * lax.rsqrt(jnp.mean(x * x, axis=-1, keepdims=True) + EPS) * w


def _iota2(shape, dim):
    return lax.broadcasted_iota(jnp.int32, shape, dim)


def _retention_kernel(*refs, layer, latent, n_chunks):
    if latent:
        (decay_ref, q_ref, k_ref, v_ref, g_ref, cos_ref, se_ref, so_ref, s0_ref,
         o_ref, q_scr, k_scr, kf_scr, kb_scr) = refs
    else:
        (decay_ref, q_ref, k_ref, v_ref, g_ref,
         o_ref, sout_ref, q_scr, k_scr, kf_scr, kb_scr) = refs
    c_len = RET_CHUNK
    h = pl.program_id(1)
    lgf = -decay_ref[layer, 0, h]
    lgb = -decay_ref[layer, 1, h]

    if latent:
        q_scr[...] = _rope(q_ref[...].astype(F32), cos_ref[...], se_ref[...], so_ref[...])
        k_scr[...] = _rope(k_ref[...].astype(F32), cos_ref[...], se_ref[...], so_ref[...]) * (HEAD_DIM ** -0.5)
    else:
        q_scr[...] = q_ref[...].astype(F32)
        k_scr[...] = k_ref[...].astype(F32) * (HEAD_DIM ** -0.5)

    ii = _iota2((c_len, c_len), 0)
    jj = _iota2((c_len, c_len), 1)
    diff = (ii - jj).astype(F32)
    tok = ii.astype(F32)
    dcomb = (jnp.where(diff >= 0, jnp.exp(lgf * jnp.maximum(diff, 0.0)), 0.0)
             + jnp.where(diff <= 0, jnp.exp(lgb * jnp.maximum(-diff, 0.0)), 0.0))
    xi_f = jnp.exp(lgf * (tok + 1.0))
    xi_b = jnp.exp(lgb * (c_len - tok))
    zeta_f = jnp.exp(lgf * (c_len - 1.0 - tok))
    zeta_b = jnp.exp(lgb * tok)
    gc_f = jnp.exp(jnp.full((c_len, c_len), c_len, F32) * lgf)
    gc_b = jnp.exp(jnp.full((c_len, c_len), c_len, F32) * lgb)

    def rows(c):
        return pl.ds(pl.multiple_of(c * c_len, c_len), c_len)

    unroll = min(RET_UNROLL, n_chunks)
    assert n_chunks % unroll == 0

    def local(it, carry):
        cs = [it * unroll + u for u in range(unroll)]
        kcs = [k_scr[rows(c), :] for c in cs]
        vcs = [v_ref[rows(c), :] for c in cs]
        for c, kc, vc in zip(cs, kcs, vcs):
            kf_scr[c] = _bdot_g(kc * zeta_f, vc, _TN)
        for c, kc, vc in zip(cs, kcs, vcs):
            kb_scr[c] = _bdot_g(kc * zeta_b, vc, _TN)
        return carry

    lax.fori_loop(0, n_chunks // unroll, local, 0)

    if latent:
        sf0 = s0_ref[0]
        sb0 = s0_ref[1]
    else:
        sf0 = jnp.zeros((c_len, c_len), F32)
        sb0 = sf0

    def scan_f(c, s):
        kl = kf_scr[c]
        kf_scr[c] = s
        return s * gc_f + kl

    def scan_b(t, s):
        c = n_chunks - 1 - t
        kl = kb_scr[c]
        kb_scr[c] = s
        return s * gc_b + kl

    sf = lax.fori_loop(0, n_chunks, scan_f, sf0)
    sb = lax.fori_loop(0, n_chunks, scan_b, sb0)
    if not latent:
        sout_ref[0] = sf
        sout_ref[1] = sb

    def output(it, carry):
        cs = [it * unroll + u for u in range(unroll)]
        qcs = [q_scr[rows(c), :] for c in cs]
        atts = [_bdot_g(qc, k_scr[rows(c), :], _NT) * dcomb for c, qc in zip(cs, qcs)]
        inter = [_bdot(jnp.concatenate([qc * xi_f, qc * xi_b], axis=1),
                       jnp.concatenate([kf_scr[c], kb_scr[c]], axis=0)) for c, qc in zip(cs, qcs)]
        os_ = [_bdot(att, v_ref[rows(c), :]) + x for c, att, x in zip(cs, atts, inter)]
        mus = [jnp.mean(o, axis=-1, keepdims=True) for o in os_]
        ds = [o - mu for o, mu in zip(os_, mus)]
        vars_ = [jnp.mean(d * d, axis=-1, keepdims=True) for d in ds]
        for c, d, var in zip(cs, ds, vars_):
            o_ref[rows(c), :] = _silu(g_ref[rows(c), :].astype(F32)) * (d * lax.rsqrt(var + EPS))
        return carry

    lax.fori_loop(0, n_chunks // unroll, output, 0)


def retention(z, ret_decay, layer, batch, seq_len, *, rope=None, state0=None):
    latent = rope is not None
    n = z.shape[0]
    t = seq_len
    n_chunks = t // RET_CHUNK
    hd = HEAD_DIM
    nh = RET_HEADS
    col = lambda base: pl.BlockSpec((t, hd), lambda b, h, base=base: (b, base + h))
    in_specs = [pl.BlockSpec(memory_space=pltpu.SMEM), col(0), col(nh), col(2 * nh), col(3 * nh)]
    args = [ret_decay, z, z, z, z]
    out_shape = [jax.ShapeDtypeStruct((n, MIX_W), F32)]
    out_specs = [pl.BlockSpec((t, hd), lambda b, h: (b, h))]
    if latent:
        tab = pl.BlockSpec((t, hd), lambda b, h: (0, 0))
        in_specs += [tab, tab, tab,
                     pl.BlockSpec((None, None, 2, None, hd, hd), lambda b, h: (b, layer, 0, h, 0, 0))]
        args += [*rope, state0]
    else:
        out_shape.append(jax.ShapeDtypeStruct((batch, 2, nh, hd, hd), F32))
        out_specs.append(pl.BlockSpec((None, 2, None, hd, hd), lambda b, h: (b, 0, h, 0, 0)))
    kern = functools.partial(_retention_kernel, layer=layer, latent=latent, n_chunks=n_chunks)
    res = pl.pallas_call(
        kern,
        out_shape=out_shape,
        grid=(batch, nh),
        in_specs=in_specs,
        out_specs=out_specs,
        scratch_shapes=[pltpu.VMEM((t, hd), F32), pltpu.VMEM((t, hd), F32),
                        pltpu.VMEM((n_chunks, hd, hd), F32), pltpu.VMEM((n_chunks, hd, hd), F32)],
        compiler_params=_cparams("arbitrary", "arbitrary"),
        name="retention",
    )(*args)
    return res if not latent else res[0]


def _attention_kernel(*refs, latent, tq, past):
    if latent:
        (q_ref, k_ref, v_ref, qn_ref, kn_ref, cos_ref, se_ref, so_ref, ck_ref, cv_ref,
         o_ref, kb_scr, vb_scr) = refs
    else:
        (q_ref, k_ref, v_ref, qn_ref, kn_ref,
         o_ref, nk_ref, nv_ref, kb_scr, vb_scr) = refs
    qi = pl.program_id(2)
    hd = HEAD_DIM

    @pl.when(qi == 0)
    def _():
        kn = _rms(k_ref[...].astype(F32), kn_ref[...])
        if latent:
            kb_scr[0:past, :] = ck_ref[...].astype(BF16)
            vb_scr[0:past, :] = cv_ref[...].astype(BF16)
            kb_scr[past:, :] = _rope(kn, cos_ref[...], se_ref[...], so_ref[...]).astype(BF16)
            vb_scr[past:, :] = v_ref[...].astype(BF16)
        else:
            kb_scr[...] = kn.astype(BF16)
            vb_scr[...] = v_ref[...].astype(BF16)
            nk_ref[...] = kn
            nv_ref[...] = v_ref[...].astype(F32)

    q = q_ref[...].astype(F32)
    heads = []
    for g in range(2):
        qg = _rms(q[:, g * hd:(g + 1) * hd], qn_ref[...])
        if latent:
            r = pl.ds(pl.multiple_of(qi * tq, tq), tq)
            qg = _rope(qg, cos_ref[r, :], se_ref[r, :], so_ref[r, :])
        heads.append(qg)
    qq = jnp.concatenate(heads, axis=0).astype(BF16)
    rc = ATT_ROW_CHUNK
    n_rc = (2 * tq) // rc
    scale = hd ** -0.5
    scores, probs, outs = {}, {}, {}

    def score(c):
        scores[c] = lax.dot_general(qq[c * rc:(c + 1) * rc], kb_scr[...], _NT, preferred_element_type=F32)

    def soft(c):
        s = scores.pop(c)
        p = jnp.exp((s - jnp.max(s, axis=-1, keepdims=True)) * scale)
        probs[c] = (p.astype(BF16), jnp.sum(p, axis=-1, keepdims=True))

    def out(c):
        p, l = probs.pop(c)
        outs[c] = jnp.dot(p, vb_scr[...], preferred_element_type=F32) / l

    score(0)
    for c in range(n_rc):
        if c + 1 < n_rc:
            score(c + 1)
        soft(c)
        if c >= 1:
            out(c - 1)
    out(n_rc - 1)
    o = jnp.concatenate([outs[c] for c in range(n_rc)], axis=0)
    o_ref[...] = jnp.concatenate([o[:tq], o[tq:]], axis=1)


def attention(z, q_norm_w, k_norm_w, layer, batch, seq_len, *, rope=None, cache_k=None, cache_v=None):
    latent = rope is not None
    n = z.shape[0]
    t = seq_len
    hd = HEAD_DIM
    tq = min(t, 256)
    past = cache_k.shape[2] if latent else 0
    q0 = (RET_COLS + RWKV_COLS) // (2 * hd)
    k0 = (RET_COLS + RWKV_COLS) // hd + ATT_HEADS
    v0 = k0 + ATT_KV_HEADS
    nq = t // tq
    in_specs = [
        pl.BlockSpec((tq, 2 * hd), lambda b, g, i: (b * nq + i, q0 + g)),
        pl.BlockSpec((t, hd), lambda b, g, i: (b, k0 + g)),
        pl.BlockSpec((t, hd), lambda b, g, i: (b, v0 + g)),
        pl.BlockSpec((None, 1, hd), lambda b, g, i: (layer, 0, 0)),
        pl.BlockSpec((None, 1, hd), lambda b, g, i: (layer, 0, 0)),
    ]
    args = [z, z, z, q_norm_w, k_norm_w]
    out_shape = [jax.ShapeDtypeStruct((n, MIX_W), F32)]
    out_specs = [pl.BlockSpec((tq, 2 * hd), lambda b, g, i: (b * nq + i, g))]
    if latent:
        tab = pl.BlockSpec((t, hd), lambda b, g, i: (0, 0))
        cache = pl.BlockSpec((None, None, past, hd), lambda b, g, i: (b, layer, 0, g))
        in_specs += [tab, tab, tab, cache, cache]
        flat = cache_k.shape[:3] + (ATT_KV_HEADS * hd,)
        args += [*rope, cache_k.reshape(flat), cache_v.reshape(flat)]
    else:
        new = jax.ShapeDtypeStruct((batch, t, ATT_KV_HEADS * hd), F32)
        out_shape += [new, new]
        spec = pl.BlockSpec((None, t, hd), lambda b, g, i: (b, 0, g))
        out_specs += [spec, spec]
    kern = functools.partial(_attention_kernel, latent=latent, tq=tq, past=past)
    res = pl.pallas_call(
        kern,
        out_shape=out_shape,
        grid=(batch, ATT_KV_HEADS, nq),
        in_specs=in_specs,
        out_specs=out_specs,
        scratch_shapes=[pltpu.VMEM((past + t, hd), BF16), pltpu.VMEM((past + t, hd), BF16)],
        compiler_params=_cparams("arbitrary", "arbitrary", "arbitrary"),
        name="attention",
    )(*args)
    return res if not latent else res[0]


def _softplus(x):
    return jnp.maximum(x, 0.0) + jnp.log1p(jnp.exp(-jnp.abs(x)))


def _token_shift(u, prev_row, next_row, mu):
    tr = u.shape[0]
    row = _iota2(u.shape, 0)
    prev = jnp.where(row == 0, prev_row, pltpu.roll(u, 1, axis=0))
    nxt = jnp.where(row == tr - 1, next_row, pltpu.roll(u, tr - 1, axis=0))
    return u + mu[0:1, :] * (prev - u) + mu[1:2, :] * (nxt - u)


def _rwkv_prep_kernel(zr, zk, zv, zl, pr, pk, pv, pl_, nr, nk, nv, nl, mu_r, mu_k, mu_v, mu_l,
                      w0_ref, wup_ref, a0_ref, aup_ref, gup_ref, kkw_ref, kaw_ref, rkw_ref, seg_ref,
                      r_o, k_o, v_o, kk_o, b_o, lw0_o, lw1_o, bonus_o, gate_o, *, tiles_per_seq):
    i = pl.program_id(0)
    pos = i % tiles_per_seq
    keep_prev = jnp.where(pos == 0, 0.0, 1.0)
    keep_next = jnp.where(pos == tiles_per_seq - 1, 0.0, 1.0)

    def shifted(cur, prv, nxt, mu):
        return _token_shift(cur[...].astype(F32), prv[Z_ROWS - 1:Z_ROWS, :].astype(F32) * keep_prev,
                            nxt[0:1, :].astype(F32) * keep_next, mu[...])

    wr = shifted(zr, pr, nr, mu_r)
    wk = shifted(zk, pk, nk, mu_k)
    wv = shifted(zv, pv, nv, mu_v)
    lo = shifted(zl, pl_, nl, mu_l)

    seg = seg_ref[...]
    icl = _sigmoid(a0_ref[...] + _dot_x3(lo, aup_ref[...]))
    gate_o[...] = _dot_x3(_sigmoid(lo), gup_ref[...])
    th = jnp.tanh(lo)
    for d, lw_o in enumerate((lw0_o, lw1_o)):
        wl = w0_ref[d:d + 1, :] + _dot_x3(th, wup_ref[d])
        lw_o[...] = -jnp.exp(-_softplus(-wl) - 0.5)
    kk = wk * kkw_ref[...]
    kk = kk * lax.rsqrt(_dot_exact_rhs(kk * kk, seg) + 1e-12)
    k = wk * (1.0 + (icl - 1.0) * kaw_ref[...])
    r_o[...] = wr
    k_o[...] = k
    v_o[...] = wv
    kk_o[...] = kk
    b_o[...] = kk * icl
    bonus_o[...] = _dot_exact_rhs(wr * k * rkw_ref[...], seg) * wv


def rwkv_prep(z, p, layer, seq_len):
    n = z.shape[0]
    tr = 256
    w = MIX_W
    c0 = RET_COLS // w
    cl = (RET_COLS + 3 * w) // RWKV_LORA
    nt = n // tr
    rb = tr // Z_ROWS
    last = n // Z_ROWS - 1
    cur = lambda width, c: pl.BlockSpec((tr, width), lambda i, c=c: (i, c))
    prv = lambda width, c: pl.BlockSpec((Z_ROWS, width), lambda i, c=c: (jnp.maximum(i * rb - 1, 0), c))
    nxt = lambda width, c: pl.BlockSpec((Z_ROWS, width), lambda i, c=c: (jnp.minimum((i + 1) * rb, last), c))
    cols = [(w, c0), (w, c0 + 1), (w, c0 + 2), (RWKV_LORA, cl)]
    mu_cols = [(w, 0), (w, 1), (w, 2), (RWKV_LORA, 3 * w // RWKV_LORA)]
    whole = lambda *shape: pl.BlockSpec((None,) + shape, lambda i: (layer,) + (0,) * len(shape))
    in_specs = ([cur(*c) for c in cols] + [prv(*c) for c in cols] + [nxt(*c) for c in cols]
                + [pl.BlockSpec((None, 2, width), lambda i, c=c: (layer, 0, c)) for width, c in mu_cols]
                + [whole(2, w), whole(2, RWKV_LORA, w), whole(1, w), whole(RWKV_LORA, w), whole(RWKV_LORA, w),
                   whole(1, w), whole(1, w), whole(1, w), pl.BlockSpec((w, w), lambda i: (0, 0))])
    out = jax.ShapeDtypeStruct((n, w), F32)
    kern = functools.partial(_rwkv_prep_kernel, tiles_per_seq=seq_len // tr)
    return pl.pallas_call(
        kern,
        out_shape=[out] * 9,
        grid=(nt,),
        in_specs=in_specs,
        out_specs=[pl.BlockSpec((tr, w), lambda i: (i, 0))] * 9,
        compiler_params=_cparams("arbitrary"),
        name="rwkv_prep",
    )(*([z] * 12), *([p["rwkv_mu"]] * 4), p["rwkv_w0"], p["w_up_pad"], p["rwkv_a0"], p["a_up_pad"], p["g_up_pad"],
      p["rwkv_k_k"], p["rwkv_k_a"], p["rwkv_r_k"], p["seg_ones"])


def _tri_inverse_steps(mats, eye, blk16, off32, off64):
    ds = [jnp.where(blk16, a, 0.0) for a in mats]
    ts = [eye + d for d in ds]
    pws = ds
    for _ in range(3):
        pws = [_bdot(pw, pw) for pw in pws]
        yield None
        ts = [_bdot(t, eye + pw) for t, pw in zip(ts, pws)]
        yield None
    for off in (off32, off64):
        halves = [_bdot(t, jnp.where(off, a, 0.0)) for t, a in zip(ts, mats)]
        yield None
        ts = [t + _bdot(h, t) for t, h in zip(ts, halves)]
        yield None
    yield ts


def _rwkv_scan_kernel(*refs, latent, n_chunks, group, pairs):
    scratch = refs[-10:]
    yf_scr, yb_scr, s_scr, ar_s, tinv_s, av_s, mrbk_s, vst_s, bko_s, gt_s = scratch
    if latent:
        (r_ref, k_ref, v_ref, kk_ref, b_ref, lw0_ref, lw1_ref, bonus_ref, gate_ref, lnw_ref, s0_ref,
         o_ref) = refs[:-10]
    else:
        (r_ref, k_ref, v_ref, kk_ref, b_ref, lw0_ref, lw1_ref, bonus_ref, gate_ref, lnw_ref,
         o_ref, sout_ref) = refs[:-10]
    cl = RWKV_CHUNK
    hn = RWKV_N
    two = 2 * cl
    m0 = _iota2((1, LANE), 1) < hn
    row = _iota2((two, two), 0)
    col = _iota2((two, two), 1)
    same = (row >= cl) == (col >= cl)
    eye = jnp.where(row == col, 1.0, 0.0)
    blk16 = (row >> 4) == (col >> 4)
    off32 = ((row >> 5) == (col >> 5)) & ~blk16
    off64 = same & ((row >> 5) != (col >> 5))
    r64 = _iota2((cl, cl), 0)
    c64 = _iota2((cl, cl), 1)
    masks = (
        (same & (row >= col), same & (row > col), jnp.where(r64 >= c64, 1.0, 0.0).astype(BF16)),
        (same & (row <= col), same & (row < col), jnp.where(r64 <= c64, 1.0, 0.0).astype(BF16)),
    )

    def stack(x):
        return jnp.concatenate([jnp.where(m0, x, 0.0), jnp.where(m0, 0.0, x)], axis=0)

    def cols(p):
        return slice(p * LANE, (p + 1) * LANE)

    zero = jnp.zeros((hn, hn), F32)
    for d in range(2):
        for p in range(pairs):
            if latent:
                s_scr[d * pairs + p] = jnp.concatenate(
                    [jnp.concatenate([s0_ref[d, 2 * p], zero], axis=1),
                     jnp.concatenate([zero, s0_ref[d, 2 * p + 1]], axis=1)], axis=0)
            else:
                s_scr[d * pairs + p] = jnp.zeros((LANE, LANE), F32)

    lw_refs = (lw0_ref, lw1_ref)
    y_scrs = (yf_scr, yb_scr)

    def chunk_rows(g, j, d):
        ci = g * group + j
        ci = ci if d == 0 else n_chunks - 1 - ci
        return pl.ds(pl.multiple_of(ci * cl, cl), cl)

    units = [(j, d, p) for j in range(group) for d in range(2) for p in range(pairs)]
    dot = functools.partial(jnp.dot, preferred_element_type=F32)

    def prepare_steps(g, slot):
        rows = [(chunk_rows(g, j, d), cols(p)) for j, d, p in units]
        lws = [lw_refs[d][r, c] for (j, d, p), (r, c) in zip(units, rows)]
        parts = [_split2(lw) for lw in lws]
        lgs = [dot(masks[d][2], hi) + dot(masks[d][2], lo) for (j, d, p), (hi, lo) in zip(units, parts)]
        yield
        tots = [lg[cl - 1:cl, :] if d == 0 else lg[0:1, :] for (j, d, p), lg in zip(units, lgs)]
        ars, bks = [], []
        for i, (r, c) in enumerate(rows):
            lg, lw, tot = lgs[i], lws[i], tots[i]
            e_neg = jnp.exp(-lg)
            e_out = jnp.exp(tot - lg)
            bb = b_ref[r, c]
            kc = k_ref[r, c]
            ar = jnp.concatenate([stack(-kk_ref[r, c] * jnp.exp(lg - lw)), stack(r_ref[r, c] * jnp.exp(lg))], axis=0)
            ars.append(ar)
            bks.append(jnp.concatenate([stack(bb * e_neg), stack(kc * e_neg)], axis=0))
            ar_s[slot, i] = ar.astype(BF16)
            bko_s[slot, i] = jnp.concatenate([stack(bb * e_out), stack(kc * e_out)], axis=0).astype(BF16)
            gt_s[slot, i] = jnp.broadcast_to(jnp.exp(tot), (SUBLANE, LANE))
        yield
        crosses = [_bdot_g(ar, bk, _NT) for ar, bk in zip(ars, bks)]
        yield
        a_abs, a_aks = [], []
        for i, ((j, d, p), cross) in enumerate(zip(units, crosses)):
            incl, strict, _ = masks[d]
            a_abs.append(jnp.where(strict, cross[:two, :two], 0.0))
            a_aks.append(jnp.where(strict, cross[:two, two:], 0.0))
            mrbk_s[slot, i] = jnp.concatenate([jnp.where(incl, cross[two:, :two], 0.0),
                                               jnp.where(incl, cross[two:, two:], 0.0)], axis=1).astype(BF16)
        v_sts = [stack(v_ref[r, c]) for r, c in rows]
        for i, (a_ak, v_st) in enumerate(zip(a_aks, v_sts)):
            vst_s[slot, i] = v_st.astype(BF16)
            av_s[slot, i] = _bdot(a_ak, v_st)
        yield
        inverses = None
        for inverses in _tri_inverse_steps(a_abs, eye, blk16, off32, off64):
            yield
        for i, t_inv in enumerate(inverses):
            tinv_s[slot, i] = t_inv.astype(BF16)

    def advance_steps(g, slot):
        for j in range(group):
            chains = [(i, d, p) for i, (uj, d, p) in enumerate(units) if uj == j]
            ss = [s_scr[d * pairs + p] for i, d, p in chains]
            arss = [_bdot_g(ar_s[slot, i], s, _NT) for (i, d, p), s in zip(chains, ss)]
            us = [dot(tinv_s[slot, i], (ars[:two] + av_s[slot, i]).astype(BF16)) for (i, d, p), ars in zip(chains, arss)]
            uvs = [jnp.concatenate([u.astype(BF16), vst_s[slot, i]], axis=0) for (i, d, p), u in zip(chains, us)]
            for (i, d, p), s, ars, uv in zip(chains, ss, arss, uvs):
                y = ars[two:] + dot(mrbk_s[slot, i], uv)
                s_scr[d * pairs + p] = s * gt_s[slot, i][0:1, :] + lax.dot_general(uv, bko_s[slot, i], _TN,
                                                                                   preferred_element_type=F32)
                y_scrs[d][chunk_rows(g, j, d), cols(p)] = y[:cl] + y[cl:]
            yield

    def run(main, side=(), every=1):
        side = iter(side)
        for n_done, _ in enumerate(main, start=1):
            if n_done % every == 0:
                next(side, None)
        for _ in side:
            pass

    n_groups = n_chunks // group
    run(prepare_steps(0, 0))
    if n_groups > 1:
        def body(g, carry):
            run(prepare_steps(g, g % 2), advance_steps(g - 1, (g - 1) % 2), every=3)
            return carry

        lax.fori_loop(1, n_groups, body, 0)
    run(advance_steps(n_groups - 1, (n_groups - 1) % 2))

    if not latent:
        for d in range(2):
            for p in range(pairs):
                s = s_scr[d * pairs + p]
                sout_ref[d, 2 * p] = s[:hn, :hn]
                sout_ref[d, 2 * p + 1] = s[hn:, hn:]

    rt = 256 if (n_chunks * cl) % 256 == 0 else n_chunks * cl

    def epilogue(j, carry):
        rows = pl.ds(pl.multiple_of(j * rt, rt), rt)
        for p in range(pairs):
            c = cols(p)
            y = yf_scr[rows, c] + yb_scr[rows, c]
            s_all = jnp.sum(y, axis=-1, keepdims=True)
            s_0 = jnp.sum(jnp.where(m0, y, 0.0), axis=-1, keepdims=True)
            mu = jnp.where(m0, s_0, s_all - s_0) * (1.0 / hn)
            dlt = y - mu
            q_all = jnp.sum(dlt * dlt, axis=-1, keepdims=True)
            q_0 = jnp.sum(jnp.where(m0, dlt * dlt, 0.0), axis=-1, keepdims=True)
            var = jnp.where(m0, q_0, q_all - q_0) * (1.0 / hn)
            yn = dlt * lax.rsqrt(var + RWKV_LN_EPS) * lnw_ref[:, c] + bonus_ref[rows, c]
            o_ref[rows, c] = yn * gate_ref[rows, c]
        return carry

    lax.fori_loop(0, (n_chunks * cl) // rt, epilogue, 0)


def rwkv_scan(ops, ln_w, layer, batch, seq_len, *, state0=None):
    latent = state0 is not None
    n = ops[0].shape[0]
    t = seq_len
    hn = RWKV_N
    n_chunks = t // RWKV_CHUNK
    group = min(RWKV_GROUP, n_chunks)
    assert n_chunks % group == 0
    pairs = 2 if n_chunks == group else 1
    npair = RWKV_HEADS // 2
    assert npair % pairs == 0
    width = pairs * LANE
    blk = pl.BlockSpec((t, width), lambda b, h: (b, h))
    in_specs = [blk] * 9 + [pl.BlockSpec((None, 1, width), lambda b, h: (layer, 0, h))]
    args = [*ops, ln_w]
    out_shape = [jax.ShapeDtypeStruct((n, MIX_W), F32)]
    out_specs = [blk]
    if latent:
        in_specs.append(pl.BlockSpec((None, None, 2, 2 * pairs, hn, hn), lambda b, h: (b, layer, 0, h, 0, 0)))
        args.append(state0)
    else:
        out_shape.append(jax.ShapeDtypeStruct((batch, 2, RWKV_HEADS, hn, hn), F32))
        out_specs.append(pl.BlockSpec((None, 2, 2 * pairs, hn, hn), lambda b, h: (b, 0, h, 0, 0)))
    units = 2 * group * pairs
    two = 2 * RWKV_CHUNK
    kern = functools.partial(_rwkv_scan_kernel, latent=latent, n_chunks=n_chunks, group=group, pairs=pairs)
    res = pl.pallas_call(
        kern,
        out_shape=out_shape,
        grid=(batch, npair // pairs),
        in_specs=in_specs,
        out_specs=out_specs,
        scratch_shapes=[
            pltpu.VMEM((t, width), F32), pltpu.VMEM((t, width), F32), pltpu.VMEM((2 * pairs, LANE, LANE), F32),
            pltpu.VMEM((2, units, 2 * two, LANE), BF16),
            pltpu.VMEM((2, units, two, two), BF16),
            pltpu.VMEM((2, units, two, LANE), F32),
            pltpu.VMEM((2, units, two, 2 * two), BF16),
            pltpu.VMEM((2, units, two, LANE), BF16),
            pltpu.VMEM((2, units, 2 * two, LANE), BF16),
            pltpu.VMEM((2, units, SUBLANE, LANE), F32),
        ],
        compiler_params=_cparams("arbitrary", "arbitrary"),
        name="rwkv_scan",
    )(*args)
    return res if not latent else res[0]


assert D_MODEL == SUBLANE * LANE


def _token_tiles(n_tok, index_map):
    return pl.BlockSpec((n_tok * SUBLANE, LANE), lambda *a: (index_map(*a), 0))


def _token_rows(t):
    return pl.ds(pl.multiple_of(t * SUBLANE, SUBLANE), SUBLANE)


def _load_token_tiles(ref, n_tok, lead=()):
    return jnp.concatenate([ref[(*lead, pl.ds(j, n_tok, stride=SUBLANE), slice(None))] for j in range(SUBLANE)],
                           axis=1)


def _store_token_tiles(ref, x):
    for j in range(SUBLANE):
        ref[pl.ds(j, x.shape[0], stride=SUBLANE), :] = x[:, j * LANE:(j + 1) * LANE]


def _topk_softmax(logits):
    lane = _iota2(logits.shape, 1)
    out_lane = _iota2((logits.shape[0], TOP_K), 1)
    n_exp = logits.shape[1]
    vals, idxs = [], []
    work = logits
    for _ in range(TOP_K):
        m = jnp.max(work, axis=-1, keepdims=True)
        idx = jnp.min(jnp.where(work == m, lane, n_exp), axis=-1, keepdims=True)
        vals.append(m)
        idxs.append(idx)
        work = jnp.where(lane == idx, -jnp.inf, work)
    es = [jnp.exp(v - vals[0]) for v in vals]
    denom = es[0] + es[1] + es[2] + es[3]
    top_i = jnp.zeros((logits.shape[0], TOP_K), jnp.int32)
    top_p = jnp.zeros((logits.shape[0], TOP_K), F32)
    for j in range(TOP_K):
        top_i = jnp.where(out_lane == j, idxs[j], top_i)
        top_p = jnp.where(out_lane == j, es[j] / denom, top_p)
    return top_i, top_p


def _merge_kernel(x_ref, oret_ref, orwkv_ref, oatt_ref, zg_ref, wb_ref, wo_ref, g1_ref, nw2_ref, sh2_ref, sc2_ref,
                  rw_ref, rb_ref, x1_ref, h2_ref, ti_ref, tp_ref, *, mod_base, tiles_per_seq):
    r = _mod_row(pl.program_id(0), mod_base, tiles_per_seq)
    d = x_ref.shape[1]
    acc = None
    for nb, o_ref in enumerate((oret_ref, orwkv_ref, oatt_ref)):
        proj = jnp.dot(o_ref[...].astype(BF16), wb_ref[nb], preferred_element_type=F32)
        term = _sigmoid(zg_ref[:, nb * d:(nb + 1) * d].astype(F32)) * proj
        acc = term if acc is None else acc + term
    out = jnp.dot(acc.astype(BF16), wo_ref[...], preferred_element_type=F32)
    x1 = x_ref[...] + g1_ref[pl.ds(r, 1), :] * out
    x1_ref[...] = x1
    h2 = _modnorm(x1, nw2_ref[...], sh2_ref[pl.ds(r, 1), :], sc2_ref[pl.ds(r, 1), :])
    _store_token_tiles(h2_ref, h2)
    logits = _dot_x3(h2, rw_ref[...]) + rb_ref[...]
    top_i, top_p = _topk_softmax(logits)
    ti_ref[...] = top_i
    tp_ref[...] = top_p


def merge(x, o_ret, o_rwkv, o_att, zg, p, mod, layer, *, seq_len, mod_base):
    n, d = x.shape
    tm = 512
    assert n % tm == 0 and (mod_base == 0 or seq_len % tm == 0)
    w = MIX_W
    row = lambda width: pl.BlockSpec((tm, width), lambda i: (i, 0))
    modc = lambda c: pl.BlockSpec((None, MOD_ROWS, d), lambda i, c=c: (layer, 0, c))
    kern = functools.partial(_merge_kernel, mod_base=mod_base, tiles_per_seq=seq_len // tm)
    return pl.pallas_call(
        kern,
        out_shape=[jax.ShapeDtypeStruct((n, d), F32), jax.ShapeDtypeStruct((n * SUBLANE, LANE), F32),
                   jax.ShapeDtypeStruct((n, TOP_K), jnp.int32), jax.ShapeDtypeStruct((n, TOP_K), F32)],
        grid=(n // tm,),
        in_specs=[
            row(d), row(w), row(w), row(w), row(N_BRANCH * d),
            pl.BlockSpec((None, N_BRANCH, w, d), lambda i: (layer, 0, 0, 0)),
            pl.BlockSpec((None, d, d), lambda i: (layer, 0, 0)),
            modc(2),
            pl.BlockSpec((None, 1, d), lambda i: (layer, 0, 0)),
            modc(3), modc(4),
            pl.BlockSpec((None, d, N_EXPERTS), lambda i: (layer, 0, 0)),
            pl.BlockSpec((None, 1, N_EXPERTS), lambda i: (layer, 0, 0)),
        ],
        out_specs=[row(d), _token_tiles(tm, lambda i: i), row(TOP_K), row(TOP_K)],
        compiler_params=_cparams("arbitrary"),
        name="merge",
    )(x, o_ret, o_rwkv, o_att, zg, p["w_branch_bf"], p["w_out_bf"], mod, p["norm2_w"], mod, mod,
      p["router_w"], p["router_b"])


def _moe_kernel(te_ref, nu_ref, xs_ref, wgu_ref, bgu_ref, wdn_ref, bdn_ref, y_ref, wgu_scr, wdn_scr):
    i = pl.program_id(0)
    prev = te_ref[jnp.maximum(i - 1, 0)]

    @pl.when((i == 0) | (te_ref[i] != prev))
    def _():
        wgu_scr[...] = wgu_ref[...].astype(BF16)
        wdn_scr[...] = wdn_ref[...].astype(BF16)

    @pl.when(i >= nu_ref[0])
    def _():
        y_ref[...] = jnp.zeros_like(y_ref)

    @pl.when(i < nu_ref[0])
    def _():
        f = wdn_scr.shape[0]
        x = _load_token_tiles(xs_ref, xs_ref.shape[0] // SUBLANE).astype(BF16)
        gu = jnp.dot(x, wgu_scr[...], preferred_element_type=F32) + bgu_ref[...]
        gate = jnp.minimum(gu[:, :f], SWIGLU_LIMIT)
        up = jnp.clip(gu[:, f:], -SWIGLU_LIMIT, SWIGLU_LIMIT)
        act = (up + 1.0) * gate * _sigmoid(SWIGLU_ALPHA * gate)
        y = jnp.dot(act.astype(BF16), wdn_scr[...], preferred_element_type=F32) + bdn_ref[...]
        _store_token_tiles(y_ref, y)


def moe_experts(xs, tile_expert, n_used, p, layer, tile):
    s = xs.shape[0] // SUBLANE
    d = D_MODEL
    f = D_FF
    grid_spec = pltpu.PrefetchScalarGridSpec(
        num_scalar_prefetch=2,
        grid=(s // tile,),
        in_specs=[
            _token_tiles(tile, lambda i, te, nu: jnp.minimum(i, nu[0] - 1)),
            pl.BlockSpec((None, None, d, 2 * f), lambda i, te, nu: (layer, te[i], 0, 0)),
            pl.BlockSpec((None, None, 1, 2 * f), lambda i, te, nu: (layer, te[i], 0, 0)),
            pl.BlockSpec((None, None, f, d), lambda i, te, nu: (layer, te[i], 0, 0)),
            pl.BlockSpec((None, None, 1, d), lambda i, te, nu: (layer, te[i], 0, 0)),
        ],
        out_specs=_token_tiles(tile, lambda i, te, nu: i),
        scratch_shapes=[pltpu.VMEM((d, 2 * f), BF16), pltpu.VMEM((f, d), BF16)],
    )
    return pl.pallas_call(
        _moe_kernel,
        out_shape=jax.ShapeDtypeStruct((s * SUBLANE, LANE), F32),
        grid_spec=grid_spec,
        compiler_params=_cparams("arbitrary"),
        name="moe_experts",
    )(tile_expert, n_used, xs, p["w_gate_up"], p["b_gate_up"], p["w_down"], p["b_down"])


def _choice_onehots(top_i):
    lane = _iota2((top_i.shape[0], N_EXPERTS), 1)
    return [jnp.where(top_i[:, k:k + 1] == lane, 1.0, 0.0) for k in range(TOP_K)]


def _per_choice(onehots, table):
    out_lane = _iota2((onehots[0].shape[0], TOP_K), 1)
    out = jnp.zeros((onehots[0].shape[0], TOP_K), F32)
    for k, oh in enumerate(onehots):
        out = jnp.where(out_lane == k, jnp.sum(oh * table, axis=-1, keepdims=True), out)
    return out


def _route_rank_kernel(ti_ref, rank_ref, cnt_ref, run_scr):
    @pl.when(pl.program_id(0) == 0)
    def _():
        run_scr[...] = jnp.zeros_like(run_scr)

    tm = ti_ref.shape[0]
    ohs = _choice_onehots(ti_ref[...])
    chosen = ohs[0] + ohs[1] + ohs[2] + ohs[3]
    earlier = jnp.where(_iota2((tm, tm), 0) > _iota2((tm, tm), 1), 1.0, 0.0).astype(BF16)
    before = jnp.dot(earlier, chosen.astype(BF16), preferred_element_type=F32) + run_scr[...]
    rank_ref[...] = _per_choice(ohs, before).astype(jnp.int32)
    run_scr[...] += jnp.sum(chosen, axis=0, keepdims=True)
    cnt_ref[...] = run_scr[...].astype(jnp.int32)


def _route_pos_kernel(ti_ref, rank_ref, off_ref, pos_ref):
    ohs = _choice_onehots(ti_ref[...])
    pos_ref[...] = rank_ref[...] + _per_choice(ohs, off_ref[...]).astype(jnp.int32)


def route(top_i, tile):
    n = top_i.shape[0]
    tm = next(t for t in (1024, 768, 512, 256) if n % t == 0)
    tok = pl.BlockSpec((tm, TOP_K), lambda i: (i, 0))
    per_e = pl.BlockSpec((1, N_EXPERTS), lambda i: (0, 0))
    rank, counts = pl.pallas_call(
        _route_rank_kernel,
        out_shape=[jax.ShapeDtypeStruct((n, TOP_K), jnp.int32), jax.ShapeDtypeStruct((1, N_EXPERTS), jnp.int32)],
        grid=(n // tm,),
        in_specs=[tok],
        out_specs=[tok, per_e],
        scratch_shapes=[pltpu.VMEM((1, N_EXPERTS), F32)],
        compiler_params=_cparams("arbitrary"),
        name="route_rank",
    )(top_i)
    counts = counts[0]
    padded = ((counts + tile - 1) // tile) * tile
    ends = jnp.cumsum(padded)
    n_tiles = (n * TOP_K) // tile + N_EXPERTS
    dump = n_tiles * tile
    tile_start = jnp.arange(n_tiles + 1, dtype=jnp.int32) * tile
    tile_expert = jnp.minimum(jnp.sum((tile_start[:, None] >= ends[None, :]).astype(jnp.int32), axis=1), N_EXPERTS - 1)
    n_used = (ends[-1:] // tile).astype(jnp.int32)
    last_tile = jnp.where(padded > 0, ends - tile, dump).astype(jnp.int32)
    pos = pl.pallas_call(
        _route_pos_kernel,
        out_shape=jax.ShapeDtypeStruct((n, TOP_K), jnp.int32),
        grid=(n // tm,),
        in_specs=[tok, tok, per_e],
        out_specs=tok,
        compiler_params=_cparams("arbitrary"),
        name="route_pos",
    )(top_i, rank, (ends - padded).astype(F32).reshape(1, N_EXPERTS))
    return pos, tile_expert.astype(jnp.int32), n_used, last_tile


def _token_copy(src_ref, src_idx, dst_ref, dst_idx, sem):
    return pltpu.make_async_copy(src_ref.at[_token_rows(src_idx), :], dst_ref.at[_token_rows(dst_idx), :], sem)


def _dispatch_kernel(pos_ref, last_ref, nu_ref, ha_ref, hb_ref, xs_ref, zero_scr, sem, *, tile, tiles_a):
    n_tok = ha_ref.shape[0] // SUBLANE

    @pl.when(pl.program_id(0) == 0)
    def _():
        zero_scr[...] = jnp.zeros_like(zero_scr)

        def fill(row):
            rows = pl.ds(pl.multiple_of(row * SUBLANE, tile * SUBLANE), tile * SUBLANE)
            return pltpu.make_async_copy(zero_scr, xs_ref.at[rows, :], sem)

        fills = [fill(last_ref[e]) for e in range(N_EXPERTS)]
        for c in fills:
            c.start()
        for c in fills:
            c.wait()

        def fill_unused(j, carry):
            c = fill(j * tile)
            c.start()
            c.wait()
            return carry

        lax.fori_loop(nu_ref[0], xs_ref.shape[0] // (tile * SUBLANE), fill_unused, 0)

    def scatter(h_ref):
        def issue(t, carry):
            for k in range(TOP_K):
                _token_copy(h_ref, t, xs_ref, pos_ref[t * TOP_K + k], sem).start(priority=k % 2)
            return carry

        def drain(t, carry):
            for k in range(TOP_K):
                _token_copy(h_ref, 0, xs_ref, 0, sem).wait()
            return carry

        lax.fori_loop(0, n_tok, issue, 0, unroll=2)
        lax.fori_loop(0, n_tok, drain, 0, unroll=2)

    @pl.when(pl.program_id(0) < tiles_a)
    def _():
        scatter(ha_ref)

    @pl.when(pl.program_id(0) >= tiles_a)
    def _():
        scatter(hb_ref)


def dispatch(h_a, h_b, pos_flat, last_tile, n_used, tile):
    n_a, n_b = h_a.shape[0] // SUBLANE, h_b.shape[0] // SUBLANE
    tm = 256
    assert n_a % tm == 0 and n_b % tm == 0
    tiles_a = n_a // tm
    n_slots = (n_a + n_b) * TOP_K + (N_EXPERTS + 1) * tile
    return pl.pallas_call(
        functools.partial(_dispatch_kernel, tile=tile, tiles_a=tiles_a),
        out_shape=jax.ShapeDtypeStruct((n_slots * SUBLANE, LANE), F32),
        grid=((n_a + n_b) // tm,),
        in_specs=[
            pl.BlockSpec((tm * TOP_K,), lambda i: (i,), memory_space=pltpu.SMEM),
            pl.BlockSpec(memory_space=pltpu.SMEM),
            pl.BlockSpec(memory_space=pltpu.SMEM),
            _token_tiles(tm, lambda i: jnp.minimum(i, tiles_a - 1)),
            _token_tiles(tm, lambda i: jnp.maximum(i - tiles_a, 0)),
        ],
        out_specs=pl.BlockSpec(memory_space=pl.ANY),
        scratch_shapes=[pltpu.VMEM((tile * SUBLANE, LANE), F32), pltpu.SemaphoreType.DMA(())],
        compiler_params=_cparams("arbitrary"),
        name="moe_dispatch",
    )(pos_flat, last_tile, n_used, h_a, h_b)


def _combine_kernel(pos_ref, pos_next_ref, x1_ref, tp_ref, g2_ref, fw_ref, ys_ref, o_ref, buf, sems,
                    *, n_tiles, mod_base, tiles_per_seq, final):
    i = pl.program_id(0)
    slot = i % 2
    tm = x1_ref.shape[0]

    def gather(p_ref, s):
        def issue(t, carry):
            for k in range(TOP_K):
                _token_copy(ys_ref, p_ref[t * TOP_K + k], buf.at[s, k], t, sems.at[s]).start(priority=k % 2)
            return carry
        lax.fori_loop(0, tm, issue, 0, unroll=2)

    @pl.when(i == 0)
    def _():
        gather(pos_ref, 0)

    @pl.when(i + 1 < n_tiles)
    def _():
        gather(pos_next_ref, 1 - slot)

    def drain(t, carry):
        for k in range(TOP_K):
            _token_copy(ys_ref, 0, buf.at[slot, 0], 0, sems.at[slot]).wait()
        return carry

    lax.fori_loop(0, tm, drain, 0, unroll=2)

    tp = tp_ref[...]
    acc = tp[:, 0:1] * _load_token_tiles(buf, tm, (slot, 0))
    for k in range(1, TOP_K):
        acc = acc + tp[:, k:k + 1] * _load_token_tiles(buf, tm, (slot, k))
    r = _mod_row(i, mod_base, tiles_per_seq)
    x2 = x1_ref[...] + g2_ref[pl.ds(r, 1), :] * acc
    o_ref[...] = _rms(x2, fw_ref[...]) if final else x2


def combine(x1, top_p, pos_flat, row0, ys, mod, final_w, layer, *, seq_len, mod_base, final):
    n, d = x1.shape
    tm = 512
    assert row0 % tm == 0 and n % tm == 0 and (mod_base == 0 or seq_len % tm == 0)
    r0 = row0 // tm
    n_tiles = n // tm
    kern = functools.partial(_combine_kernel, n_tiles=n_tiles, mod_base=mod_base, tiles_per_seq=seq_len // tm,
                             final=final)
    return pl.pallas_call(
        kern,
        out_shape=jax.ShapeDtypeStruct((n, d), F32),
        grid=(n_tiles,),
        in_specs=[
            pl.BlockSpec((tm * TOP_K,), lambda i: (r0 + i,), memory_space=pltpu.SMEM),
            pl.BlockSpec((tm * TOP_K,), lambda i: (r0 + jnp.minimum(i + 1, n_tiles - 1),), memory_space=pltpu.SMEM),
            pl.BlockSpec((tm, d), lambda i: (i, 0)),
            pl.BlockSpec((tm, TOP_K), lambda i: (i, 0)),
            pl.BlockSpec((None, MOD_ROWS, d), lambda i: (layer, 0, 5)),
            pl.BlockSpec((1, d), lambda i: (0, 0)),
            pl.BlockSpec(memory_space=pl.ANY),
        ],
        out_specs=pl.BlockSpec((tm, d), lambda i: (i, 0)),
        scratch_shapes=[pltpu.VMEM((2, TOP_K, tm * SUBLANE, LANE), F32), pltpu.SemaphoreType.DMA((2,))],
        compiler_params=_cparams("arbitrary"),
        name="moe_combine",
    )(pos_flat, pos_flat, x1, top_p, mod, final_w, ys)


MOE_TILE = 256


def _rope_tables(t):
    rows = t // GRID_W
    row = jnp.repeat(jnp.arange(rows, dtype=F32), GRID_W)
    col = jnp.tile(jnp.arange(GRID_W, dtype=F32), rows)
    n_freq = HEAD_DIM // 4
    inv = ROPE_THETA ** (-jnp.arange(n_freq, dtype=F32) / n_freq)
    ang = jnp.concatenate([row[:, None] * inv, col[:, None] * inv], axis=-1)
    cos = jnp.repeat(jnp.cos(ang), 2, axis=-1)
    sin = jnp.repeat(jnp.sin(ang), 2, axis=-1)
    even = (jnp.arange(HEAD_DIM) % 2 == 0)[None, :]
    return cos, jnp.where(even, -sin, 0.0), jnp.where(even, 0.0, sin)


def _layer_params(a, n_layers):
    d = D_MODEL
    w = MIX_W
    p = dict(a)
    for name in ("norm1_w", "norm2_w"):
        p[name] = a[name].reshape(n_layers, 1, d)
    for name in ("rwkv_a0", "rwkv_k_k", "rwkv_k_a", "rwkv_ln_w"):
        p[name] = a[name].reshape(n_layers, 1, w)
    p["rwkv_r_k"] = a["rwkv_r_k"].reshape(n_layers, 1, w)
    p["q_norm_w"] = a["q_norm_w"].reshape(n_layers, 1, HEAD_DIM)
    p["k_norm_w"] = a["k_norm_w"].reshape(n_layers, 1, HEAD_DIM)
    wr, ar, gr = a["rwkv_w_up"].shape[2], a["rwkv_a_up"].shape[1], a["rwkv_g_up"].shape[1]
    assert wr + ar + gr == RWKV_LORA
    p["w_up_pad"] = jnp.pad(a["rwkv_w_up"], ((0, 0), (0, 0), (0, ar + gr), (0, 0)))
    p["a_up_pad"] = jnp.pad(a["rwkv_a_up"], ((0, 0), (wr, gr), (0, 0)))
    p["g_up_pad"] = jnp.pad(a["rwkv_g_up"], ((0, 0), (wr + ar, 0), (0, 0)))
    head = jnp.arange(w, dtype=jnp.int32) // RWKV_N
    p["seg_ones"] = (head[:, None] == head[None, :]).astype(BF16)
    p["w_branch_bf"] = a["w_branch"].astype(BF16)
    p["w_out_bf"] = a["w_out"].astype(BF16)
    p["router_b"] = a["router_b"].reshape(n_layers, 1, N_EXPERTS)
    p["b_gate_up"] = a["b_gate_up"].reshape(n_layers, N_EXPERTS, 1, 2 * D_FF)
    p["b_down"] = a["b_down"].reshape(n_layers, N_EXPERTS, 1, d)
    return p


def _mixer(x, p, mod, layer, batch, seq_len, mod_base, latent_inputs):
    latent = latent_inputs is not None
    z = in_proj(x, p["norm1_w"], mod, p["w_in"], layer, 0, MAIN_COLS, seq_len=seq_len, mod_base=mod_base)
    zg = in_proj(x, p["norm1_w"], mod, p["w_in"], layer, MAIN_COLS, GATE_COLS, seq_len=seq_len, mod_base=mod_base)
    ops = rwkv_prep(z, p, layer, seq_len)
    if latent:
        rope, cache_k, cache_v, state_ret, state_rwkv = latent_inputs
        o_ret = retention(z, p["ret_decay"], layer, batch, seq_len, rope=rope, state0=state_ret)
        o_rwkv = rwkv_scan(ops, p["rwkv_ln_w"], layer, batch, seq_len, state0=state_rwkv)
        o_att = attention(z, p["q_norm_w"], p["k_norm_w"], layer, batch, seq_len,
                          rope=rope, cache_k=cache_k, cache_v=cache_v)
        ctx_out = None
    else:
        o_ret, s_ret = retention(z, p["ret_decay"], layer, batch, seq_len)
        o_rwkv, s_rwkv = rwkv_scan(ops, p["rwkv_ln_w"], layer, batch, seq_len)
        o_att, new_k, new_v = attention(z, p["q_norm_w"], p["k_norm_w"], layer, batch, seq_len)
        ctx_out = (new_k, new_v, s_ret, s_rwkv)
    merged = merge(x, o_ret, o_rwkv, o_att, zg, p, mod, layer, seq_len=seq_len, mod_base=mod_base)
    return merged, ctx_out


def kernel(x_prompt, x_sample, cache_k, cache_v, state_ret, state_rwkv, c, c_ctx, norm1_w, norm2_w, mod_w, mod_b, w_in, ret_decay, rwkv_mu, rwkv_w0, rwkv_w_up, rwkv_a0, rwkv_a_up, rwkv_g_up, rwkv_k_k, rwkv_k_a, rwkv_r_k, rwkv_ln_w, q_norm_w, k_norm_w, w_branch, w_out, router_w, router_b, w_gate_up, b_gate_up, w_down, b_down, final_norm_w):
    bp, tp, d = x_prompt.shape
    bs, ts, _ = x_sample.shape
    n_layers = w_in.shape[0]
    assert bs + 1 <= MOD_ROWS
    p = _layer_params(dict(
        norm1_w=norm1_w, norm2_w=norm2_w, w_in=w_in, ret_decay=ret_decay, rwkv_mu=rwkv_mu, rwkv_w0=rwkv_w0,
        rwkv_w_up=rwkv_w_up, rwkv_a0=rwkv_a0, rwkv_a_up=rwkv_a_up, rwkv_g_up=rwkv_g_up, rwkv_k_k=rwkv_k_k,
        rwkv_k_a=rwkv_k_a, rwkv_r_k=rwkv_r_k, rwkv_ln_w=rwkv_ln_w, q_norm_w=q_norm_w, k_norm_w=k_norm_w,
        w_branch=w_branch, w_out=w_out, router_w=router_w, router_b=router_b, w_gate_up=w_gate_up,
        b_gate_up=b_gate_up, w_down=w_down, b_down=b_down), n_layers)
    cond_rows = jnp.zeros((MOD_ROWS, d), F32).at[0].set(c_ctx).at[1:1 + bs].set(c)
    mod = modulation(cond_rows, mod_w, mod_b)
    rope = _rope_tables(ts)
    final_w = final_norm_w.reshape(1, d)
    np_, ns = bp * tp, bs * ts

    xp = x_prompt.reshape(np_, d)
    xs = x_sample.reshape(ns, d)
    ks_, vs_, srs, sws = [], [], [], []
    for layer in range(n_layers):
        (x1p, h2p, tip, tpp), (k_l, v_l, sr_l, sw_l) = _mixer(xp, p, mod, layer, bp, tp, 0, None)
        (x1s, h2s, tis, tps), _ = _mixer(xs, p, mod, layer, bs, ts, 1,
                                         (rope, cache_k, cache_v, state_ret, state_rwkv))
        ks_.append(k_l.reshape(bp, tp, ATT_KV_HEADS, HEAD_DIM))
        vs_.append(v_l.reshape(bp, tp, ATT_KV_HEADS, HEAD_DIM))
        srs.append(sr_l)
        sws.append(sw_l)
        pos, tile_expert, n_used, last_tile = route(jnp.concatenate([tip, tis], axis=0), MOE_TILE)
        pos_flat = pos.reshape(-1)
        xs_sorted = dispatch(h2p, h2s, pos_flat, last_tile, n_used, MOE_TILE)
        ys = moe_experts(xs_sorted, tile_expert, n_used, p, layer, MOE_TILE)
        final = layer == n_layers - 1
        xp = combine(x1p, tpp, pos_flat, 0, ys, mod, final_w, layer, seq_len=tp, mod_base=0, final=final)
        xs = combine(x1s, tps, pos_flat, np_, ys, mod, final_w, layer, seq_len=ts, mod_base=1, final=final)
    return (xp.reshape(bp, tp, d), xs.reshape(bs, ts, d),
            jnp.stack(ks_, axis=1), jnp.stack(vs_, axis=1), jnp.stack(srs, axis=1), jnp.stack(sws, axis=1))
```

```python
import functools

import jax
import jax.numpy as jnp
from jax import lax
from jax.experimental import pallas as pl
from jax.experimental.pallas import tpu as pltpu

F32 = jnp.float32
BF16 = jnp.bfloat16
Z_DTYPE = BF16
Z_ROWS = 16

D_MODEL = 1024
DEPTH = 2
GRID_W = 64
MIX_W = D_MODEL // 2
N_BRANCH = 3
EPS = 1e-6
HEAD_DIM = 128
ATT_HEADS = MIX_W // HEAD_DIM
ATT_KV_HEADS = ATT_HEADS // 2
ROPE_THETA = 10000.0
RET_HEADS = MIX_W // HEAD_DIM
RET_CHUNK = 128
RET_UNROLL = 4
ATT_ROW_CHUNK = 128
RWKV_N = 64
RWKV_HEADS = MIX_W // RWKV_N
RWKV_CHUNK = 64
RWKV_GROUP = 8
RWKV_LORA = 256
RWKV_LN_EPS = 64e-5
N_EXPERTS = 32
TOP_K = 4
D_FF = D_MODEL
SWIGLU_LIMIT = 7.0
SWIGLU_ALPHA = 1.702
RET_COLS = 4 * MIX_W
RWKV_COLS = 3 * MIX_W + RWKV_LORA
ATT_COLS = (ATT_HEADS + 2 * ATT_KV_HEADS) * HEAD_DIM
GATE_COLS = N_BRANCH * D_MODEL
MAIN_COLS = RET_COLS + RWKV_COLS + ATT_COLS
N_MOD = 6
MOD_ROWS = 8

LANE = 128
SUBLANE = 8
VMEM_LIMIT = 56 * 1024 * 1024

_NT = (((1,), (1,)), ((), ()))
_TN = (((0,), (0,)), ((), ()))


def _cparams(*sem):
    return pltpu.CompilerParams(dimension_semantics=sem, vmem_limit_bytes=VMEM_LIMIT)


def _bdot(a, b):
    return jnp.dot(a.astype(BF16), b.astype(BF16), preferred_element_type=F32)


def _bdot_g(a, b, dims):
    return lax.dot_general(a.astype(BF16), b.astype(BF16), dims, preferred_element_type=F32)


def _split2(x):
    hi = x.astype(BF16)
    lo = (x - hi.astype(F32)).astype(BF16)
    return hi, lo


def _dot_x3(a, b):
    ah = a.astype(BF16)
    bh = b.astype(BF16)
    al = (a - ah.astype(F32)).astype(BF16)
    bl = (b - bh.astype(F32)).astype(BF16)
    d = functools.partial(jnp.dot, preferred_element_type=F32)
    return d(ah, bh) + (d(ah, bl) + d(al, bh))


def _dot_exact_rhs(a, b_exact):
    hi, lo = _split2(a)
    d = functools.partial(jnp.dot, preferred_element_type=F32)
    return d(hi, b_exact) + d(lo, b_exact)


def _sigmoid(x):
    return 1.0 / (1.0 + jnp.exp(-x))


def _silu(x):
    return x * _sigmoid(x)


def _mod_kernel(cond_ref, w_ref, b_ref, o_ref):
    s = _silu(cond_ref[...])
    o_ref[...] = _dot_x3(s, w_ref[...]) + b_ref[...]


def modulation(cond_rows, mod_w, mod_b):
    n_layers, d, cols = mod_w.shape
    tn = d
    return pl.pallas_call(
        _mod_kernel,
        out_shape=jax.ShapeDtypeStruct((n_layers, MOD_ROWS, cols), F32),
        grid=(n_layers, cols // tn),
        in_specs=[
            pl.BlockSpec((MOD_ROWS, d), lambda l, j: (0, 0)),
            pl.BlockSpec((None, d, tn), lambda l, j: (l, 0, j)),
            pl.BlockSpec((None, 1, tn), lambda l, j: (l, 0, j)),
        ],
        out_specs=pl.BlockSpec((None, MOD_ROWS, tn), lambda l, j: (l, 0, j)),
        compiler_params=_cparams("arbitrary", "arbitrary"),
        name="modulation",
    )(cond_rows, mod_w, mod_b.reshape(n_layers, 1, cols))


def _mod_row(i, mod_base, tiles_per_seq):
    if mod_base == 0:
        return 0
    return mod_base + i // tiles_per_seq


def _modnorm(x, nw, shift, scale):
    ms = jnp.mean(x * x, axis=-1, keepdims=True)
    return (x * lax.rsqrt(ms + EPS)) * nw * (1.0 + scale) + shift


def _in_proj_kernel(x_ref, nw_ref, sh_ref, sc_ref, w_ref, o_ref, h_scr, *, mod_base, chunk, chunks_per_seq):
    i = pl.program_id(0)
    chunks = x_ref.shape[0] // chunk

    @pl.when(pl.program_id(1) == 0)
    def _():
        def norm_chunk(c, carry):
            r = _mod_row(i * chunks + c, mod_base, chunks_per_seq)
            rows = pl.ds(pl.multiple_of(c * chunk, chunk), chunk)
            h = _modnorm(x_ref[rows, :], nw_ref[...], sh_ref[pl.ds(r, 1), :], sc_ref[pl.ds(r, 1), :])
            h_scr[rows, :] = h.astype(BF16)
            return carry

        lax.fori_loop(0, chunks, norm_chunk, 0)

    o_ref[...] = jnp.dot(h_scr[...], w_ref[...].astype(BF16), preferred_element_type=F32).astype(o_ref.dtype)


def in_proj(x, norm_w, mod, w_in, layer, col0, ncols, *, seq_len, mod_base):
    n, d = x.shape
    tm = min(n, 4096)
    tn = 256
    chunk = min(tm, 512)
    assert n % tm == 0 and ncols % tn == 0 and col0 % tn == 0 and (mod_base == 0 or seq_len % chunk == 0)
    j0 = col0 // tn
    kern = functools.partial(_in_proj_kernel, mod_base=mod_base, chunk=chunk, chunks_per_seq=seq_len // chunk)
    return pl.pallas_call(
        kern,
        out_shape=jax.ShapeDtypeStruct((n, ncols), Z_DTYPE),
        grid=(n // tm, ncols // tn),
        in_specs=[
            pl.BlockSpec((tm, d), lambda i, j: (i, 0)),
            pl.BlockSpec((None, 1, d), lambda i, j: (layer, 0, 0)),
            pl.BlockSpec((None, MOD_ROWS, d), lambda i, j: (layer, 0, 0)),
            pl.BlockSpec((None, MOD_ROWS, d), lambda i, j: (layer, 0, 1)),
            pl.BlockSpec((None, d, tn), lambda i, j: (layer, 0, j0 + j)),
        ],
        out_specs=pl.BlockSpec((tm, tn), lambda i, j: (i, j)),
        scratch_shapes=[pltpu.VMEM((tm, d), BF16)],
        compiler_params=_cparams("arbitrary", "arbitrary"),
        name="in_proj",
    )(x, norm_w, mod, mod, w_in)


def _rope(x, cos, sin_even, sin_odd):
    nxt = pltpu.roll(x, LANE - 1, axis=1)
    prv = pltpu.roll(x, 1, axis=1)
    return x * cos + nxt * sin_even + prv * sin_odd


def _rms(x, w):
    return x * lax.rsqrt(jnp.mean(x * x, axis=-1, keepdims=True) + EPS) * w


def _iota2(shape, dim):
    return lax.broadcasted_iota(jnp.int32, shape, dim)


def _retention_kernel(*refs, layer, latent, n_chunks):
    if latent:
        (decay_ref, q_ref, k_ref, v_ref, g_ref, cos_ref, se_ref, so_ref, s0_ref,
         o_ref, q_scr, k_scr, kf_scr, kb_scr) = refs
    else:
        (decay_ref, q_ref, k_ref, v_ref, g_ref,
         o_ref, sout_ref, q_scr, k_scr, kf_scr, kb_scr) = refs
    c_len = RET_CHUNK
    h = pl.program_id(1)
    lgf = -decay_ref[layer, 0, h]
    lgb = -decay_ref[layer, 1, h]

    if latent:
        q_scr[...] = _rope(q_ref[...].astype(F32), cos_ref[...], se_ref[...], so_ref[...])
        k_scr[...] = _rope(k_ref[...].astype(F32), cos_ref[...], se_ref[...], so_ref[...]) * (HEAD_DIM ** -0.5)
    else:
        q_scr[...] = q_ref[...].astype(F32)
        k_scr[...] = k_ref[...].astype(F32) * (HEAD_DIM ** -0.5)

    ii = _iota2((c_len, c_len), 0)
    jj = _iota2((c_len, c_len), 1)
    diff = (ii - jj).astype(F32)
    tok = ii.astype(F32)
    dcomb = (jnp.where(diff >= 0, jnp.exp(lgf * jnp.maximum(diff, 0.0)), 0.0)
             + jnp.where(diff <= 0, jnp.exp(lgb * jnp.maximum(-diff, 0.0)), 0.0))
    xi_f = jnp.exp(lgf * (tok + 1.0))
    xi_b = jnp.exp(lgb * (c_len - tok))
    zeta_f = jnp.exp(lgf * (c_len - 1.0 - tok))
    zeta_b = jnp.exp(lgb * tok)
    gc_f = jnp.exp(jnp.full((c_len, c_len), c_len, F32) * lgf)
    gc_b = jnp.exp(jnp.full((c_len, c_len), c_len, F32) * lgb)

    def rows(c):
        return pl.ds(pl.multiple_of(c * c_len, c_len), c_len)

    unroll = min(RET_UNROLL, n_chunks)
    assert n_chunks % unroll == 0

    def local(it, carry):
        cs = [it * unroll + u for u in range(unroll)]
        kcs = [k_scr[rows(c), :] for c in cs]
        vcs = [v_ref[rows(c), :] for c in cs]
        for c, kc, vc in zip(cs, kcs, vcs):
            kf_scr[c] = _bdot_g(kc * zeta_f, vc, _TN)
        for c, kc, vc in zip(cs, kcs, vcs):
            kb_scr[c] = _bdot_g(kc * zeta_b, vc, _TN)
        return carry

    lax.fori_loop(0, n_chunks // unroll, local, 0)

    if latent:
        sf0 = s0_ref[0]
        sb0 = s0_ref[1]
    else:
        sf0 = jnp.zeros((c_len, c_len), F32)
        sb0 = sf0

    def scan_f(c, s):
        kl = kf_scr[c]
        kf_scr[c] = s
        return s * gc_f + kl

    def scan_b(t, s):
        c = n_chunks - 1 - t
        kl = kb_scr[c]
        kb_scr[c] = s
        return s * gc_b + kl

    sf = lax.fori_loop(0, n_chunks, scan_f, sf0)
    sb = lax.fori_loop(0, n_chunks, scan_b, sb0)
    if not latent:
        sout_ref[0] = sf
        sout_ref[1] = sb

    def output(it, carry):
        cs = [it * unroll + u for u in range(unroll)]
        qcs = [q_scr[rows(c), :] for c in cs]
        atts = [_bdot_g(qc, k_scr[rows(c), :], _NT) * dcomb for c, qc in zip(cs, qcs)]
        inter = [_bdot(jnp.concatenate([qc * xi_f, qc * xi_b], axis=1),
                       jnp.concatenate([kf_scr[c], kb_scr[c]], axis=0)) for c, qc in zip(cs, qcs)]
        os_ = [_bdot(att, v_ref[rows(c), :]) + x for c, att, x in zip(cs, atts, inter)]
        mus = [jnp.mean(o, axis=-1, keepdims=True) for o in os_]
        ds = [o - mu for o, mu in zip(os_, mus)]
        vars_ = [jnp.mean(d * d, axis=-1, keepdims=True) for d in ds]
        for c, d, var in zip(cs, ds, vars_):
            o_ref[rows(c), :] = _silu(g_ref[rows(c), :].astype(F32)) * (d * lax.rsqrt(var + EPS))
        return carry

    lax.fori_loop(0, n_chunks // unroll, output, 0)


def retention(z, ret_decay, layer, batch, seq_len, *, rope=None, state0=None):
    latent = rope is not None
    n = z.shape[0]
    t = seq_len
    n_chunks = t // RET_CHUNK
    hd = HEAD_DIM
    nh = RET_HEADS
    col = lambda base: pl.BlockSpec((t, hd), lambda b, h, base=base: (b, base + h))
    in_specs = [pl.BlockSpec(memory_space=pltpu.SMEM), col(0), col(nh), col(2 * nh), col(3 * nh)]
    args = [ret_decay, z, z, z, z]
    out_shape = [jax.ShapeDtypeStruct((n, MIX_W), F32)]
    out_specs = [pl.BlockSpec((t, hd), lambda b, h: (b, h))]
    if latent:
        tab = pl.BlockSpec((t, hd), lambda b, h: (0, 0))
        in_specs += [tab, tab, tab,
                     pl.BlockSpec((None, None, 2, None, hd, hd), lambda b, h: (b, layer, 0, h, 0, 0))]
        args += [*rope, state0]
    else:
        out_shape.append(jax.ShapeDtypeStruct((batch, 2, nh, hd, hd), F32))
        out_specs.append(pl.BlockSpec((None, 2, None, hd, hd), lambda b, h: (b, 0, h, 0, 0)))
    kern = functools.partial(_retention_kernel, layer=layer, latent=latent, n_chunks=n_chunks)
    res = pl.pallas_call(
        kern,
        out_shape=out_shape,
        grid=(batch, nh),
        in_specs=in_specs,
        out_specs=out_specs,
        scratch_shapes=[pltpu.VMEM((t, hd), F32), pltpu.VMEM((t, hd), F32),
                        pltpu.VMEM((n_chunks, hd, hd), F32), pltpu.VMEM((n_chunks, hd, hd), F32)],
        compiler_params=_cparams("arbitrary", "arbitrary"),
        name="retention",
    )(*args)
    return res if not latent else res[0]


def _attention_kernel(*refs, latent, tq, past):
    if latent:
        (q_ref, k_ref, v_ref, qn_ref, kn_ref, cos_ref, se_ref, so_ref, ck_ref, cv_ref,
         o_ref, kb_scr, vb_scr) = refs
    else:
        (q_ref, k_ref, v_ref, qn_ref, kn_ref,
         o_ref, nk_ref, nv_ref, kb_scr, vb_scr) = refs
    qi = pl.program_id(2)
    hd = HEAD_DIM

    @pl.when(qi == 0)
    def _():
        kn = _rms(k_ref[...].astype(F32), kn_ref[...])
        if latent:
            kb_scr[0:past, :] = ck_ref[...].astype(BF16)
            vb_scr[0:past, :] = cv_ref[...].astype(BF16)
            kb_scr[past:, :] = _rope(kn, cos_ref[...], se_ref[...], so_ref[...]).astype(BF16)
            vb_scr[past:, :] = v_ref[...].astype(BF16)
        else:
            kb_scr[...] = kn.astype(BF16)
            vb_scr[...] = v_ref[...].astype(BF16)
            nk_ref[...] = kn
            nv_ref[...] = v_ref[...].astype(F32)

    q = q_ref[...].astype(F32)
    heads = []
    for g in range(2):
        qg = _rms(q[:, g * hd:(g + 1) * hd], qn_ref[...])
        if latent:
            r = pl.ds(pl.multiple_of(qi * tq, tq), tq)
            qg = _rope(qg, cos_ref[r, :], se_ref[r, :], so_ref[r, :])
        heads.append(qg)
    qq = jnp.concatenate(heads, axis=0).astype(BF16)
    rc = ATT_ROW_CHUNK
    n_rc = (2 * tq) // rc
    scale = hd ** -0.5
    scores, probs, outs = {}, {}, {}

    def score(c):
        scores[c] = lax.dot_general(qq[c * rc:(c + 1) * rc], kb_scr[...], _NT, preferred_element_type=F32)

    def soft(c):
        s = scores.pop(c)
        p = jnp.exp((s - jnp.max(s, axis=-1, keepdims=True)) * scale)
        probs[c] = (p.astype(BF16), jnp.sum(p, axis=-1, keepdims=True))

    def out(c):
        p, l = probs.pop(c)
        outs[c] = jnp.dot(p, vb_scr[...], preferred_element_type=F32) / l

    score(0)
    for c in range(n_rc):
        if c + 1 < n_rc:
            score(c + 1)
        soft(c)
        if c >= 1:
            out(c - 1)
    out(n_rc - 1)
    o = jnp.concatenate([outs[c] for c in range(n_rc)], axis=0)
    o_ref[...] = jnp.concatenate([o[:tq], o[tq:]], axis=1)


def attention(z, q_norm_w, k_norm_w, layer, batch, seq_len, *, rope=None, cache_k=None, cache_v=None):
    latent = rope is not None
    n = z.shape[0]
    t = seq_len
    hd = HEAD_DIM
    tq = min(t, 256)
    past = cache_k.shape[2] if latent else 0
    q0 = (RET_COLS + RWKV_COLS) // (2 * hd)
    k0 = (RET_COLS + RWKV_COLS) // hd + ATT_HEADS
    v0 = k0 + ATT_KV_HEADS
    nq = t // tq
    in_specs = [
        pl.BlockSpec((tq, 2 * hd), lambda b, g, i: (b * nq + i, q0 + g)),
        pl.BlockSpec((t, hd), lambda b, g, i: (b, k0 + g)),
        pl.BlockSpec((t, hd), lambda b, g, i: (b, v0 + g)),
        pl.BlockSpec((None, 1, hd), lambda b, g, i: (layer, 0, 0)),
        pl.BlockSpec((None, 1, hd), lambda b, g, i: (layer, 0, 0)),
    ]
    args = [z, z, z, q_norm_w, k_norm_w]
    out_shape = [jax.ShapeDtypeStruct((n, MIX_W), F32)]
    out_specs = [pl.BlockSpec((tq, 2 * hd), lambda b, g, i: (b * nq + i, g))]
    if latent:
        tab = pl.BlockSpec((t, hd), lambda b, g, i: (0, 0))
        cache = pl.BlockSpec((None, None, past, hd), lambda b, g, i: (b, layer, 0, g))
        in_specs += [tab, tab, tab, cache, cache]
        flat = cache_k.shape[:3] + (ATT_KV_HEADS * hd,)
        args += [*rope, cache_k.reshape(flat), cache_v.reshape(flat)]
    else:
        new = jax.ShapeDtypeStruct((batch, t, ATT_KV_HEADS * hd), F32)
        out_shape += [new, new]
        spec = pl.BlockSpec((None, t, hd), lambda b, g, i: (b, 0, g))
        out_specs += [spec, spec]
    kern = functools.partial(_attention_kernel, latent=latent, tq=tq, past=past)
    res = pl.pallas_call(
        kern,
        out_shape=out_shape,
        grid=(batch, ATT_KV_HEADS, nq),
        in_specs=in_specs,
        out_specs=out_specs,
        scratch_shapes=[pltpu.VMEM((past + t, hd), BF16), pltpu.VMEM((past + t, hd), BF16)],
        compiler_params=_cparams("arbitrary", "arbitrary", "arbitrary"),
        name="attention",
    )(*args)
    return res if not latent else res[0]


def _softplus(x):
    return jnp.maximum(x, 0.0) + jnp.log1p(jnp.exp(-jnp.abs(x)))


def _token_shift(u, prev_row, next_row, mu):
    tr = u.shape[0]
    row = _iota2(u.shape, 0)
    prev = jnp.where(row == 0, prev_row, pltpu.roll(u, 1, axis=0))
    nxt = jnp.where(row == tr - 1, next_row, pltpu.roll(u, tr - 1, axis=0))
    return u + mu[0:1, :] * (prev - u) + mu[1:2, :] * (nxt - u)


def _rwkv_prep_kernel(zr, zk, zv, zl, pr, pk, pv, pl_, nr, nk, nv, nl, mu_r, mu_k, mu_v, mu_l,
                      w0_ref, wup_ref, a0_ref, aup_ref, gup_ref, kkw_ref, kaw_ref, rkw_ref, seg_ref,
                      r_o, k_o, v_o, kk_o, b_o, lw0_o, lw1_o, bonus_o, gate_o, *, tiles_per_seq):
    i = pl.program_id(0)
    pos = i % tiles_per_seq
    keep_prev = jnp.where(pos == 0, 0.0, 1.0)
    keep_next = jnp.where(pos == tiles_per_seq - 1, 0.0, 1.0)

    def shifted(cur, prv, nxt, mu):
        return _token_shift(cur[...].astype(F32), prv[Z_ROWS - 1:Z_ROWS, :].astype(F32) * keep_prev,
                            nxt[0:1, :].astype(F32) * keep_next, mu[...])

    wr = shifted(zr, pr, nr, mu_r)
    wk = shifted(zk, pk, nk, mu_k)
    wv = shifted(zv, pv, nv, mu_v)
    lo = shifted(zl, pl_, nl, mu_l)

    seg = seg_ref[...]
    icl = _sigmoid(a0_ref[...] + _dot_x3(lo, aup_ref[...]))
    gate_o[...] = _dot_x3(_sigmoid(lo), gup_ref[...])
    th = jnp.tanh(lo)
    for d, lw_o in enumerate((lw0_o, lw1_o)):
        wl = w0_ref[d:d + 1, :] + _dot_x3(th, wup_ref[d])
        lw_o[...] = -jnp.exp(-_softplus(-wl) - 0.5)
    kk = wk * kkw_ref[...]
    kk = kk * lax.rsqrt(_dot_exact_rhs(kk * kk, seg) + 1e-12)
    k = wk * (1.0 + (icl - 1.0) * kaw_ref[...])
    r_o[...] = wr
    k_o[...] = k
    v_o[...] = wv
    kk_o[...] = kk
    b_o[...] = kk * icl
    bonus_o[...] = _dot_exact_rhs(wr * k * rkw_ref[...], seg) * wv


def rwkv_prep(z, p, layer, seq_len):
    n = z.shape[0]
    tr = 256
    w = MIX_W
    c0 = RET_COLS // w
    cl = (RET_COLS + 3 * w) // RWKV_LORA
    nt = n // tr
    rb = tr // Z_ROWS
    last = n // Z_ROWS - 1
    cur = lambda width, c: pl.BlockSpec((tr, width), lambda i, c=c: (i, c))
    prv = lambda width, c: pl.BlockSpec((Z_ROWS, width), lambda i, c=c: (jnp.maximum(i * rb - 1, 0), c))
    nxt = lambda width, c: pl.BlockSpec((Z_ROWS, width), lambda i, c=c: (jnp.minimum((i + 1) * rb, last), c))
    cols = [(w, c0), (w, c0 + 1), (w, c0 + 2), (RWKV_LORA, cl)]
    mu_cols = [(w, 0), (w, 1), (w, 2), (RWKV_LORA, 3 * w // RWKV_LORA)]
    whole = lambda *shape: pl.BlockSpec((None,) + shape, lambda i: (layer,) + (0,) * len(shape))
    in_specs = ([cur(*c) for c in cols] + [prv(*c) for c in cols] + [nxt(*c) for c in cols]
                + [pl.BlockSpec((None, 2, width), lambda i, c=c: (layer, 0, c)) for width, c in mu_cols]
                + [whole(2, w), whole(2, RWKV_LORA, w), whole(1, w), whole(RWKV_LORA, w), whole(RWKV_LORA, w),
                   whole(1, w), whole(1, w), whole(1, w), pl.BlockSpec((w, w), lambda i: (0, 0))])
    out = jax.ShapeDtypeStruct((n, w), F32)
    kern = functools.partial(_rwkv_prep_kernel, tiles_per_seq=seq_len // tr)
    return pl.pallas_call(
        kern,
        out_shape=[out] * 9,
        grid=(nt,),
        in_specs=in_specs,
        out_specs=[pl.BlockSpec((tr, w), lambda i: (i, 0))] * 9,
        compiler_params=_cparams("arbitrary"),
        name="rwkv_prep",
    )(*([z] * 12), *([p["rwkv_mu"]] * 4), p["rwkv_w0"], p["w_up_pad"], p["rwkv_a0"], p["a_up_pad"], p["g_up_pad"],
      p["rwkv_k_k"], p["rwkv_k_a"], p["rwkv_r_k"], p["seg_ones"])


def _tri_inverse_steps(mats, eye, blk16, off32, off64):
    ds = [jnp.where(blk16, a, 0.0) for a in mats]
    ts = [eye + d for d in ds]
    pws = ds
    for _ in range(3):
        pws = [_bdot(pw, pw) for pw in pws]
        yield None
        ts = [_bdot(t, eye + pw) for t, pw in zip(ts, pws)]
        yield None
    for off in (off32, off64):
        halves = [_bdot(t, jnp.where(off, a, 0.0)) for t, a in zip(ts, mats)]
        yield None
        ts = [t + _bdot(h, t) for t, h in zip(ts, halves)]
        yield None
    yield ts


def _rwkv_scan_kernel(*refs, latent, n_chunks, group, pairs):
    scratch = refs[-10:]
    yf_scr, yb_scr, s_scr, ar_s, tinv_s, av_s, mrbk_s, vst_s, bko_s, gt_s = scratch
    if latent:
        (r_ref, k_ref, v_ref, kk_ref, b_ref, lw0_ref, lw1_ref, bonus_ref, gate_ref, lnw_ref, s0_ref,
         o_ref) = refs[:-10]
    else:
        (r_ref, k_ref, v_ref, kk_ref, b_ref, lw0_ref, lw1_ref, bonus_ref, gate_ref, lnw_ref,
         o_ref, sout_ref) = refs[:-10]
    cl = RWKV_CHUNK
    hn = RWKV_N
    two = 2 * cl
    m0 = _iota2((1, LANE), 1) < hn
    row = _iota2((two, two), 0)
    col = _iota2((two, two), 1)
    same = (row >= cl) == (col >= cl)
    eye = jnp.where(row == col, 1.0, 0.0)
    blk16 = (row >> 4) == (col >> 4)
    off32 = ((row >> 5) == (col >> 5)) & ~blk16
    off64 = same & ((row >> 5) != (col >> 5))
    r64 = _iota2((cl, cl), 0)
    c64 = _iota2((cl, cl), 1)
    masks = (
        (same & (row >= col), same & (row > col), jnp.where(r64 >= c64, 1.0, 0.0).astype(BF16)),
        (same & (row <= col), same & (row < col), jnp.where(r64 <= c64, 1.0, 0.0).astype(BF16)),
    )

    def stack(x):
        return jnp.concatenate([jnp.where(m0, x, 0.0), jnp.where(m0, 0.0, x)], axis=0)

    def cols(p):
        return slice(p * LANE, (p + 1) * LANE)

    zero = jnp.zeros((hn, hn), F32)
    for d in range(2):
        for p in range(pairs):
            if latent:
                s_scr[d * pairs + p] = jnp.concatenate(
                    [jnp.concatenate([s0_ref[d, 2 * p], zero], axis=1),
                     jnp.concatenate([zero, s0_ref[d, 2 * p + 1]], axis=1)], axis=0)
            else:
                s_scr[d * pairs + p] = jnp.zeros((LANE, LANE), F32)

    lw_refs = (lw0_ref, lw1_ref)
    y_scrs = (yf_scr, yb_scr)

    def chunk_rows(g, j, d):
        ci = g * group + j
        ci = ci if d == 0 else n_chunks - 1 - ci
        return pl.ds(pl.multiple_of(ci * cl, cl), cl)

    units = [(j, d, p) for j in range(group) for d in range(2) for p in range(pairs)]
    dot = functools.partial(jnp.dot, preferred_element_type=F32)

    def prepare_steps(g, slot):
        rows = [(chunk_rows(g, j, d), cols(p)) for j, d, p in units]
        lws = [lw_refs[d][r, c] for (j, d, p), (r, c) in zip(units, rows)]
        parts = [_split2(lw) for lw in lws]
        lgs = [dot(masks[d][2], hi) + dot(masks[d][2], lo) for (j, d, p), (hi, lo) in zip(units, parts)]
        yield
        tots = [lg[cl - 1:cl, :] if d == 0 else lg[0:1, :] for (j, d, p), lg in zip(units, lgs)]
        ars, bks = [], []
        for i, (r, c) in enumerate(rows):
            lg, lw, tot = lgs[i], lws[i], tots[i]
            e_neg = jnp.exp(-lg)
            e_out = jnp.exp(tot - lg)
            bb = b_ref[r, c]
            kc = k_ref[r, c]
            ar = jnp.concatenate([stack(-kk_ref[r, c] * jnp.exp(lg - lw)), stack(r_ref[r, c] * jnp.exp(lg))], axis=0)
            ars.append(ar)
            bks.append(jnp.concatenate([stack(bb * e_neg), stack(kc * e_neg)], axis=0))
            ar_s[slot, i] = ar.astype(BF16)
            bko_s[slot, i] = jnp.concatenate([stack(bb * e_out), stack(kc * e_out)], axis=0).astype(BF16)
            gt_s[slot, i] = jnp.broadcast_to(jnp.exp(tot), (SUBLANE, LANE))
        yield
        crosses = [_bdot_g(ar, bk, _NT) for ar, bk in zip(ars, bks)]
        yield
        a_abs, a_aks = [], []
        for i, ((j, d, p), cross) in enumerate(zip(units, crosses)):
            incl, strict, _ = masks[d]
            a_abs.append(jnp.where(strict, cross[:two, :two], 0.0))
            a_aks.append(jnp.where(strict, cross[:two, two:], 0.0))
            mrbk_s[slot, i] = jnp.concatenate([jnp.where(incl, cross[two:, :two], 0.0),
                                               jnp.where(incl, cross[two:, two:], 0.0)], axis=1).astype(BF16)
        v_sts = [stack(v_ref[r, c]) for r, c in rows]
        for i, (a_ak, v_st) in enumerate(zip(a_aks, v_sts)):
            vst_s[slot, i] = v_st.astype(BF16)
            av_s[slot, i] = _bdot(a_ak, v_st)
        yield
        inverses = None
        for inverses in _tri_inverse_steps(a_abs, eye, blk16, off32, off64):
            yield
        for i, t_inv in enumerate(inverses):
            tinv_s[slot, i] = t_inv.astype(BF16)

    def advance_steps(g, slot):
        for j in range(group):
            chains = [(i, d, p) for i, (uj, d, p) in enumerate(units) if uj == j]
            ss = [s_scr[d * pairs + p] for i, d, p in chains]
            arss = [_bdot_g(ar_s[slot, i], s, _NT) for (i, d, p), s in zip(chains, ss)]
            us = [dot(tinv_s[slot, i], (ars[:two] + av_s[slot, i]).astype(BF16)) for (i, d, p), ars in zip(chains, arss)]
            uvs = [jnp.concatenate([u.astype(BF16), vst_s[slot, i]], axis=0) for (i, d, p), u in zip(chains, us)]
            for (i, d, p), s, ars, uv in zip(chains, ss, arss, uvs):
                y = ars[two:] + dot(mrbk_s[slot, i], uv)
                s_scr[d * pairs + p] = s * gt_s[slot, i][0:1, :] + lax.dot_general(uv, bko_s[slot, i], _TN,
                                                                                   preferred_element_type=F32)
                y_scrs[d][chunk_rows(g, j, d), cols(p)] = y[:cl] + y[cl:]
            yield

    def run(main, side=(), every=1):
        side = iter(side)
        for n_done, _ in enumerate(main, start=1):
            if n_done % every == 0:
                next(side, None)
        for _ in side:
            pass

    n_groups = n_chunks // group
    run(prepare_steps(0, 0))
    if n_groups > 1:
        def body(g, carry):
            run(prepare_steps(g, g % 2), advance_steps(g - 1, (g - 1) % 2), every=max(1, 14 // group))
            return carry

        lax.fori_loop(1, n_groups, body, 0)
    run(advance_steps(n_groups - 1, (n_groups - 1) % 2))

    if not latent:
        for d in range(2):
            for p in range(pairs):
                s = s_scr[d * pairs + p]
                sout_ref[d, 2 * p] = s[:hn, :hn]
                sout_ref[d, 2 * p + 1] = s[hn:, hn:]

    rt = 256 if (n_chunks * cl) % 256 == 0 else n_chunks * cl

    def epilogue(j, carry):
        rows = pl.ds(pl.multiple_of(j * rt, rt), rt)
        for p in range(pairs):
            c = cols(p)
            y = yf_scr[rows, c] + yb_scr[rows, c]
            s_all = jnp.sum(y, axis=-1, keepdims=True)
            s_0 = jnp.sum(jnp.where(m0, y, 0.0), axis=-1, keepdims=True)
            mu = jnp.where(m0, s_0, s_all - s_0) * (1.0 / hn)
            dlt = y - mu
            q_all = jnp.sum(dlt * dlt, axis=-1, keepdims=True)
            q_0 = jnp.sum(jnp.where(m0, dlt * dlt, 0.0), axis=-1, keepdims=True)
            var = jnp.where(m0, q_0, q_all - q_0) * (1.0 / hn)
            yn = dlt * lax.rsqrt(var + RWKV_LN_EPS) * lnw_ref[:, c] + bonus_ref[rows, c]
            o_ref[rows, c] = yn * gate_ref[rows, c]
        return carry

    lax.fori_loop(0, (n_chunks * cl) // rt, epilogue, 0)


def rwkv_scan(ops, ln_w, layer, batch, seq_len, *, state0=None):
    latent = state0 is not None
    n = ops[0].shape[0]
    t = seq_len
    hn = RWKV_N
    n_chunks = t // RWKV_CHUNK
    group = min(RWKV_GROUP, n_chunks)
    assert n_chunks % group == 0
    pairs = 2 if n_chunks == group else 1
    npair = RWKV_HEADS // 2
    assert npair % pairs == 0
    width = pairs * LANE
    blk = pl.BlockSpec((t, width), lambda b, h: (b, h))
    in_specs = [blk] * 9 + [pl.BlockSpec((None, 1, width), lambda b, h: (layer, 0, h))]
    args = [*ops, ln_w]
    out_shape = [jax.ShapeDtypeStruct((n, MIX_W), F32)]
    out_specs = [blk]
    if latent:
        in_specs.append(pl.BlockSpec((None, None, 2, 2 * pairs, hn, hn), lambda b, h: (b, layer, 0, h, 0, 0)))
        args.append(state0)
    else:
        out_shape.append(jax.ShapeDtypeStruct((batch, 2, RWKV_HEADS, hn, hn), F32))
        out_specs.append(pl.BlockSpec((None, 2, 2 * pairs, hn, hn), lambda b, h: (b, 0, h, 0, 0)))
    units = 2 * group * pairs
    two = 2 * RWKV_CHUNK
    kern = functools.partial(_rwkv_scan_kernel, latent=latent, n_chunks=n_chunks, group=group, pairs=pairs)
    res = pl.pallas_call(
        kern,
        out_shape=out_shape,
        grid=(batch, npair // pairs),
        in_specs=in_specs,
        out_specs=out_specs,
        scratch_shapes=[
            pltpu.VMEM((t, width), F32), pltpu.VMEM((t, width), F32), pltpu.VMEM((2 * pairs, LANE, LANE), F32),
            pltpu.VMEM((2, units, 2 * two, LANE), BF16),
            pltpu.VMEM((2, units, two, two), BF16),
            pltpu.VMEM((2, units, two, LANE), F32),
            pltpu.VMEM((2, units, two, 2 * two), BF16),
            pltpu.VMEM((2, units, two, LANE), BF16),
            pltpu.VMEM((2, units, 2 * two, LANE), BF16),
            pltpu.VMEM((2, units, SUBLANE, LANE), F32),
        ],
        compiler_params=_cparams("arbitrary", "arbitrary"),
        name="rwkv_scan",
    )(*args)
    return res if not latent else res[0]


assert D_MODEL == SUBLANE * LANE


def _token_tiles(n_tok, index_map):
    return pl.BlockSpec((n_tok * SUBLANE, LANE), lambda *a: (index_map(*a), 0))


def _token_rows(t):
    return pl.ds(pl.multiple_of(t * SUBLANE, SUBLANE), SUBLANE)


def _load_token_tiles(ref, n_tok, lead=()):
    return jnp.concatenate([ref[(*lead, pl.ds(j, n_tok, stride=SUBLANE), slice(None))] for j in range(SUBLANE)],
                           axis=1)


def _store_token_tiles(ref, x):
    for j in range(SUBLANE):
        ref[pl.ds(j, x.shape[0], stride=SUBLANE), :] = x[:, j * LANE:(j + 1) * LANE]


def _topk_softmax(logits):
    lane = _iota2(logits.shape, 1)
    out_lane = _iota2((logits.shape[0], TOP_K), 1)
    n_exp = logits.shape[1]
    vals, idxs = [], []
    work = logits
    for _ in range(TOP_K):
        m = jnp.max(work, axis=-1, keepdims=True)
        idx = jnp.min(jnp.where(work == m, lane, n_exp), axis=-1, keepdims=True)
        vals.append(m)
        idxs.append(idx)
        work = jnp.where(lane == idx, -jnp.inf, work)
    es = [jnp.exp(v - vals[0]) for v in vals]
    denom = es[0] + es[1] + es[2] + es[3]
    top_i = jnp.zeros((logits.shape[0], TOP_K), jnp.int32)
    top_p = jnp.zeros((logits.shape[0], TOP_K), F32)
    for j in range(TOP_K):
        top_i = jnp.where(out_lane == j, idxs[j], top_i)
        top_p = jnp.where(out_lane == j, es[j] / denom, top_p)
    return top_i, top_p


def _merge_kernel(x_ref, oret_ref, orwkv_ref, oatt_ref, zg_ref, wb_ref, wo_ref, g1_ref, nw2_ref, sh2_ref, sc2_ref,
                  rw_ref, rb_ref, x1_ref, h2_ref, ti_ref, tp_ref, *, mod_base, tiles_per_seq):
    r = _mod_row(pl.program_id(0), mod_base, tiles_per_seq)
    d = x_ref.shape[1]
    acc = None
    for nb, o_ref in enumerate((oret_ref, orwkv_ref, oatt_ref)):
        proj = jnp.dot(o_ref[...].astype(BF16), wb_ref[nb], preferred_element_type=F32)
        term = _sigmoid(zg_ref[:, nb * d:(nb + 1) * d].astype(F32)) * proj
        acc = term if acc is None else acc + term
    out = jnp.dot(acc.astype(BF16), wo_ref[...], preferred_element_type=F32)
    x1 = x_ref[...] + g1_ref[pl.ds(r, 1), :] * out
    x1_ref[...] = x1
    h2 = _modnorm(x1, nw2_ref[...], sh2_ref[pl.ds(r, 1), :], sc2_ref[pl.ds(r, 1), :])
    _store_token_tiles(h2_ref, h2)
    logits = _dot_x3(h2, rw_ref[...]) + rb_ref[...]
    top_i, top_p = _topk_softmax(logits)
    ti_ref[...] = top_i
    tp_ref[...] = top_p


def merge(x, o_ret, o_rwkv, o_att, zg, p, mod, layer, *, seq_len, mod_base):
    n, d = x.shape
    tm = 512
    assert n % tm == 0 and (mod_base == 0 or seq_len % tm == 0)
    w = MIX_W
    row = lambda width: pl.BlockSpec((tm, width), lambda i: (i, 0))
    modc = lambda c: pl.BlockSpec((None, MOD_ROWS, d), lambda i, c=c: (layer, 0, c))
    kern = functools.partial(_merge_kernel, mod_base=mod_base, tiles_per_seq=seq_len // tm)
    return pl.pallas_call(
        kern,
        out_shape=[jax.ShapeDtypeStruct((n, d), F32), jax.ShapeDtypeStruct((n * SUBLANE, LANE), F32),
                   jax.ShapeDtypeStruct((n, TOP_K), jnp.int32), jax.ShapeDtypeStruct((n, TOP_K), F32)],
        grid=(n // tm,),
        in_specs=[
            row(d), row(w), row(w), row(w), row(N_BRANCH * d),
            pl.BlockSpec((None, N_BRANCH, w, d), lambda i: (layer, 0, 0, 0)),
            pl.BlockSpec((None, d, d), lambda i: (layer, 0, 0)),
            modc(2),
            pl.BlockSpec((None, 1, d), lambda i: (layer, 0, 0)),
            modc(3), modc(4),
            pl.BlockSpec((None, d, N_EXPERTS), lambda i: (layer, 0, 0)),
            pl.BlockSpec((None, 1, N_EXPERTS), lambda i: (layer, 0, 0)),
        ],
        out_specs=[row(d), _token_tiles(tm, lambda i: i), row(TOP_K), row(TOP_K)],
        compiler_params=_cparams("arbitrary"),
        name="merge",
    )(x, o_ret, o_rwkv, o_att, zg, p["w_branch_bf"], p["w_out_bf"], mod, p["norm2_w"], mod, mod,
      p["router_w"], p["router_b"])


def _moe_kernel(te_ref, nu_ref, xs_ref, wgu_ref, bgu_ref, wdn_ref, bdn_ref, y_ref, wgu_scr, wdn_scr):
    i = pl.program_id(0)
    prev = te_ref[jnp.maximum(i - 1, 0)]

    @pl.when((i == 0) | (te_ref[i] != prev))
    def _():
        wgu_scr[...] = wgu_ref[...].astype(BF16)
        wdn_scr[...] = wdn_ref[...].astype(BF16)

    @pl.when(i >= nu_ref[0])
    def _():
        y_ref[...] = jnp.zeros_like(y_ref)

    @pl.when(i < nu_ref[0])
    def _():
        f = wdn_scr.shape[0]
        x = _load_token_tiles(xs_ref, xs_ref.shape[0] // SUBLANE).astype(BF16)
        gu = jnp.dot(x, wgu_scr[...], preferred_element_type=F32) + bgu_ref[...]
        gate = jnp.minimum(gu[:, :f], SWIGLU_LIMIT)
        up = jnp.clip(gu[:, f:], -SWIGLU_LIMIT, SWIGLU_LIMIT)
        act = (up + 1.0) * gate * _sigmoid(SWIGLU_ALPHA * gate)
        y = jnp.dot(act.astype(BF16), wdn_scr[...], preferred_element_type=F32) + bdn_ref[...]
        _store_token_tiles(y_ref, y)


def moe_experts(xs, tile_expert, n_used, p, layer, tile):
    s = xs.shape[0] // SUBLANE
    d = D_MODEL
    f = D_FF
    grid_spec = pltpu.PrefetchScalarGridSpec(
        num_scalar_prefetch=2,
        grid=(s // tile,),
        in_specs=[
            _token_tiles(tile, lambda i, te, nu: jnp.minimum(i, nu[0] - 1)),
            pl.BlockSpec((None, None, d, 2 * f), lambda i, te, nu: (layer, te[i], 0, 0)),
            pl.BlockSpec((None, None, 1, 2 * f), lambda i, te, nu: (layer, te[i], 0, 0)),
            pl.BlockSpec((None, None, f, d), lambda i, te, nu: (layer, te[i], 0, 0)),
            pl.BlockSpec((None, None, 1, d), lambda i, te, nu: (layer, te[i], 0, 0)),
        ],
        out_specs=_token_tiles(tile, lambda i, te, nu: i),
        scratch_shapes=[pltpu.VMEM((d, 2 * f), BF16), pltpu.VMEM((f, d), BF16)],
    )
    return pl.pallas_call(
        _moe_kernel,
        out_shape=jax.ShapeDtypeStruct((s * SUBLANE, LANE), F32),
        grid_spec=grid_spec,
        compiler_params=_cparams("arbitrary"),
        name="moe_experts",
    )(tile_expert, n_used, xs, p["w_gate_up"], p["b_gate_up"], p["w_down"], p["b_down"])


def _choice_onehots(top_i):
    lane = _iota2((top_i.shape[0], N_EXPERTS), 1)
    return [jnp.where(top_i[:, k:k + 1] == lane, 1.0, 0.0) for k in range(TOP_K)]


def _per_choice(onehots, table):
    out_lane = _iota2((onehots[0].shape[0], TOP_K), 1)
    out = jnp.zeros((onehots[0].shape[0], TOP_K), F32)
    for k, oh in enumerate(onehots):
        out = jnp.where(out_lane == k, jnp.sum(oh * table, axis=-1, keepdims=True), out)
    return out


def _route_rank_kernel(ti_ref, rank_ref, cnt_ref, run_scr):
    @pl.when(pl.program_id(0) == 0)
    def _():
        run_scr[...] = jnp.zeros_like(run_scr)

    tm = ti_ref.shape[0]
    ohs = _choice_onehots(ti_ref[...])
    chosen = ohs[0] + ohs[1] + ohs[2] + ohs[3]
    earlier = jnp.where(_iota2((tm, tm), 0) > _iota2((tm, tm), 1), 1.0, 0.0).astype(BF16)
    before = jnp.dot(earlier, chosen.astype(BF16), preferred_element_type=F32) + run_scr[...]
    rank_ref[...] = _per_choice(ohs, before).astype(jnp.int32)
    run_scr[...] += jnp.sum(chosen, axis=0, keepdims=True)
    cnt_ref[...] = run_scr[...].astype(jnp.int32)


def _route_pos_kernel(ti_ref, rank_ref, off_ref, pos_ref):
    ohs = _choice_onehots(ti_ref[...])
    pos_ref[...] = rank_ref[...] + _per_choice(ohs, off_ref[...]).astype(jnp.int32)


def route(top_i, tile):
    n = top_i.shape[0]
    tm = next(t for t in (1024, 768, 512, 256) if n % t == 0)
    tok = pl.BlockSpec((tm, TOP_K), lambda i: (i, 0))
    per_e = pl.BlockSpec((1, N_EXPERTS), lambda i: (0, 0))
    rank, counts = pl.pallas_call(
        _route_rank_kernel,
        out_shape=[jax.ShapeDtypeStruct((n, TOP_K), jnp.int32), jax.ShapeDtypeStruct((1, N_EXPERTS), jnp.int32)],
        grid=(n // tm,),
        in_specs=[tok],
        out_specs=[tok, per_e],
        scratch_shapes=[pltpu.VMEM((1, N_EXPERTS), F32)],
        compiler_params=_cparams("arbitrary"),
        name="route_rank",
    )(top_i)
    counts = counts[0]
    padded = ((counts + tile - 1) // tile) * tile
    ends = jnp.cumsum(padded)
    n_tiles = (n * TOP_K) // tile + N_EXPERTS
    dump = n_tiles * tile
    tile_start = jnp.arange(n_tiles + 1, dtype=jnp.int32) * tile
    tile_expert = jnp.minimum(jnp.sum((tile_start[:, None] >= ends[None, :]).astype(jnp.int32), axis=1), N_EXPERTS - 1)
    n_used = (ends[-1:] // tile).astype(jnp.int32)
    last_tile = jnp.where(padded > 0, ends - tile, dump).astype(jnp.int32)
    pos = pl.pallas_call(
        _route_pos_kernel,
        out_shape=jax.ShapeDtypeStruct((n, TOP_K), jnp.int32),
        grid=(n // tm,),
        in_specs=[tok, tok, per_e],
        out_specs=tok,
        compiler_params=_cparams("arbitrary"),
        name="route_pos",
    )(top_i, rank, (ends - padded).astype(F32).reshape(1, N_EXPERTS))
    return pos, tile_expert.astype(jnp.int32), n_used, last_tile


def _token_copy(src_ref, src_idx, dst_ref, dst_idx, sem):
    return pltpu.make_async_copy(src_ref.at[_token_rows(src_idx), :], dst_ref.at[_token_rows(dst_idx), :], sem)


def _dispatch_kernel(pos_ref, last_ref, nu_ref, ha_ref, hb_ref, xs_ref, zero_scr, sem, *, tile, tiles_a):
    n_tok = ha_ref.shape[0] // SUBLANE

    @pl.when(pl.program_id(0) == 0)
    def _():
        zero_scr[...] = jnp.zeros_like(zero_scr)

        def fill(row):
            rows = pl.ds(pl.multiple_of(row * SUBLANE, tile * SUBLANE), tile * SUBLANE)
            return pltpu.make_async_copy(zero_scr, xs_ref.at[rows, :], sem)

        fills = [fill(last_ref[e]) for e in range(N_EXPERTS)]
        for c in fills:
            c.start()
        for c in fills:
            c.wait()

        def fill_unused(j, carry):
            c = fill(j * tile)
            c.start()
            c.wait()
            return carry

        lax.fori_loop(nu_ref[0], xs_ref.shape[0] // (tile * SUBLANE), fill_unused, 0)

    def scatter(h_ref):
        def issue(t, carry):
            for k in range(TOP_K):
                _token_copy(h_ref, t, xs_ref, pos_ref[t * TOP_K + k], sem).start(priority=k % 2)
            return carry

        def drain(t, carry):
            for k in range(TOP_K):
                _token_copy(h_ref, 0, xs_ref, 0, sem).wait()
            return carry

        lax.fori_loop(0, n_tok, issue, 0, unroll=2)
        lax.fori_loop(0, n_tok, drain, 0, unroll=2)

    @pl.when(pl.program_id(0) < tiles_a)
    def _():
        scatter(ha_ref)

    @pl.when(pl.program_id(0) >= tiles_a)
    def _():
        scatter(hb_ref)


def dispatch(h_a, h_b, pos_flat, last_tile, n_used, tile):
    n_a, n_b = h_a.shape[0] // SUBLANE, h_b.shape[0] // SUBLANE
    tm = 256
    assert n_a % tm == 0 and n_b % tm == 0
    tiles_a = n_a // tm
    n_slots = (n_a + n_b) * TOP_K + (N_EXPERTS + 1) * tile
    return pl.pallas_call(
        functools.partial(_dispatch_kernel, tile=tile, tiles_a=tiles_a),
        out_shape=jax.ShapeDtypeStruct((n_slots * SUBLANE, LANE), F32),
        grid=((n_a + n_b) // tm,),
        in_specs=[
            pl.BlockSpec((tm * TOP_K,), lambda i: (i,), memory_space=pltpu.SMEM),
            pl.BlockSpec(memory_space=pltpu.SMEM),
            pl.BlockSpec(memory_space=pltpu.SMEM),
            _token_tiles(tm, lambda i: jnp.minimum(i, tiles_a - 1)),
            _token_tiles(tm, lambda i: jnp.maximum(i - tiles_a, 0)),
        ],
        out_specs=pl.BlockSpec(memory_space=pl.ANY),
        scratch_shapes=[pltpu.VMEM((tile * SUBLANE, LANE), F32), pltpu.SemaphoreType.DMA(())],
        compiler_params=_cparams("arbitrary"),
        name="moe_dispatch",
    )(pos_flat, last_tile, n_used, h_a, h_b)


def _combine_kernel(pos_ref, pos_next_ref, x1_ref, tp_ref, g2_ref, fw_ref, ys_ref, o_ref, buf, sems,
                    *, n_tiles, mod_base, tiles_per_seq, final):
    i = pl.program_id(0)
    slot = i % 2
    tm = x1_ref.shape[0]

    def gather(p_ref, s):
        def issue(t, carry):
            for k in range(TOP_K):
                _token_copy(ys_ref, p_ref[t * TOP_K + k], buf.at[s, k], t, sems.at[s]).start(priority=k % 2)
            return carry
        lax.fori_loop(0, tm, issue, 0, unroll=2)

    @pl.when(i == 0)
    def _():
        gather(pos_ref, 0)

    @pl.when(i + 1 < n_tiles)
    def _():
        gather(pos_next_ref, 1 - slot)

    def drain(t, carry):
        for k in range(TOP_K):
            _token_copy(ys_ref, 0, buf.at[slot, 0], 0, sems.at[slot]).wait()
        return carry

    lax.fori_loop(0, tm, drain, 0, unroll=2)

    tp = tp_ref[...]
    acc = tp[:, 0:1] * _load_token_tiles(buf, tm, (slot, 0))
    for k in range(1, TOP_K):
        acc = acc + tp[:, k:k + 1] * _load_token_tiles(buf, tm, (slot, k))
    r = _mod_row(i, mod_base, tiles_per_seq)
    x2 = x1_ref[...] + g2_ref[pl.ds(r, 1), :] * acc
    o_ref[...] = _rms(x2, fw_ref[...]) if final else x2


def combine(x1, top_p, pos_flat, row0, ys, mod, final_w, layer, *, seq_len, mod_base, final):
    n, d = x1.shape
    tm = 512
    assert row0 % tm == 0 and n % tm == 0 and (mod_base == 0 or seq_len % tm == 0)
    r0 = row0 // tm
    n_tiles = n // tm
    kern = functools.partial(_combine_kernel, n_tiles=n_tiles, mod_base=mod_base, tiles_per_seq=seq_len // tm,
                             final=final)
    return pl.pallas_call(
        kern,
        out_shape=jax.ShapeDtypeStruct((n, d), F32),
        grid=(n_tiles,),
        in_specs=[
            pl.BlockSpec((tm * TOP_K,), lambda i: (r0 + i,), memory_space=pltpu.SMEM),
            pl.BlockSpec((tm * TOP_K,), lambda i: (r0 + jnp.minimum(i + 1, n_tiles - 1),), memory_space=pltpu.SMEM),
            pl.BlockSpec((tm, d), lambda i: (i, 0)),
            pl.BlockSpec((tm, TOP_K), lambda i: (i, 0)),
            pl.BlockSpec((None, MOD_ROWS, d), lambda i: (layer, 0, 5)),
            pl.BlockSpec((1, d), lambda i: (0, 0)),
            pl.BlockSpec(memory_space=pl.ANY),
        ],
        out_specs=pl.BlockSpec((tm, d), lambda i: (i, 0)),
        scratch_shapes=[pltpu.VMEM((2, TOP_K, tm * SUBLANE, LANE), F32), pltpu.SemaphoreType.DMA((2,))],
        compiler_params=_cparams("arbitrary"),
        name="moe_combine",
    )(pos_flat, pos_flat, x1, top_p, mod, final_w, ys)


MOE_TILE = 256


def _rope_tables(t):
    rows = t // GRID_W
    row = jnp.repeat(jnp.arange(rows, dtype=F32), GRID_W)
    col = jnp.tile(jnp.arange(GRID_W, dtype=F32), rows)
    n_freq = HEAD_DIM // 4
    inv = ROPE_THETA ** (-jnp.arange(n_freq, dtype=F32) / n_freq)
    ang = jnp.concatenate([row[:, None] * inv, col[:, None] * inv], axis=-1)
    cos = jnp.repeat(jnp.cos(ang), 2, axis=-1)
    sin = jnp.repeat(jnp.sin(ang), 2, axis=-1)
    even = (jnp.arange(HEAD_DIM) % 2 == 0)[None, :]
    return cos, jnp.where(even, -sin, 0.0), jnp.where(even, 0.0, sin)


def _layer_params(a, n_layers):
    d = D_MODEL
    w = MIX_W
    p = dict(a)
    for name in ("norm1_w", "norm2_w"):
        p[name] = a[name].reshape(n_layers, 1, d)
    for name in ("rwkv_a0", "rwkv_k_k", "rwkv_k_a", "rwkv_ln_w"):
        p[name] = a[name].reshape(n_layers, 1, w)
    p["rwkv_r_k"] = a["rwkv_r_k"].reshape(n_layers, 1, w)
    p["q_norm_w"] = a["q_norm_w"].reshape(n_layers, 1, HEAD_DIM)
    p["k_norm_w"] = a["k_norm_w"].reshape(n_layers, 1, HEAD_DIM)
    wr, ar, gr = a["rwkv_w_up"].shape[2], a["rwkv_a_up"].shape[1], a["rwkv_g_up"].shape[1]
    assert wr + ar + gr == RWKV_LORA
    p["w_up_pad"] = jnp.pad(a["rwkv_w_up"], ((0, 0), (0, 0), (0, ar + gr), (0, 0)))
    p["a_up_pad"] = jnp.pad(a["rwkv_a_up"], ((0, 0), (wr, gr), (0, 0)))
    p["g_up_pad"] = jnp.pad(a["rwkv_g_up"], ((0, 0), (wr + ar, 0), (0, 0)))
    head = jnp.arange(w, dtype=jnp.int32) // RWKV_N
    p["seg_ones"] = (head[:, None] == head[None, :]).astype(BF16)
    p["w_branch_bf"] = a["w_branch"].astype(BF16)
    p["w_out_bf"] = a["w_out"].astype(BF16)
    p["router_b"] = a["router_b"].reshape(n_layers, 1, N_EXPERTS)
    p["b_gate_up"] = a["b_gate_up"].reshape(n_layers, N_EXPERTS, 1, 2 * D_FF)
    p["b_down"] = a["b_down"].reshape(n_layers, N_EXPERTS, 1, d)
    return p


def _mixer(x, p, mod, layer, batch, seq_len, mod_base, latent_inputs):
    latent = latent_inputs is not None
    z = in_proj(x, p["norm1_w"], mod, p["w_in"], layer, 0, MAIN_COLS, seq_len=seq_len, mod_base=mod_base)
    zg = in_proj(x, p["norm1_w"], mod, p["w_in"], layer, MAIN_COLS, GATE_COLS, seq_len=seq_len, mod_base=mod_base)
    ops = rwkv_prep(z, p, layer, seq_len)
    if latent:
        rope, cache_k, cache_v, state_ret, state_rwkv = latent_inputs
        o_ret = retention(z, p["ret_decay"], layer, batch, seq_len, rope=rope, state0=state_ret)
        o_rwkv = rwkv_scan(ops, p["rwkv_ln_w"], layer, batch, seq_len, state0=state_rwkv)
        o_att = attention(z, p["q_norm_w"], p["k_norm_w"], layer, batch, seq_len,
                          rope=rope, cache_k=cache_k, cache_v=cache_v)
        ctx_out = None
    else:
        o_ret, s_ret = retention(z, p["ret_decay"], layer, batch, seq_len)
        o_rwkv, s_rwkv = rwkv_scan(ops, p["rwkv_ln_w"], layer, batch, seq_len)
        o_att, new_k, new_v = attention(z, p["q_norm_w"], p["k_norm_w"], layer, batch, seq_len)
        ctx_out = (new_k, new_v, s_ret, s_rwkv)
    merged = merge(x, o_ret, o_rwkv, o_att, zg, p, mod, layer, seq_len=seq_len, mod_base=mod_base)
    return merged, ctx_out


def kernel(x_prompt, x_sample, cache_k, cache_v, state_ret, state_rwkv, c, c_ctx, norm1_w, norm2_w, mod_w, mod_b, w_in, ret_decay, rwkv_mu, rwkv_w0, rwkv_w_up, rwkv_a0, rwkv_a_up, rwkv_g_up, rwkv_k_k, rwkv_k_a, rwkv_r_k, rwkv_ln_w, q_norm_w, k_norm_w, w_branch, w_out, router_w, router_b, w_gate_up, b_gate_up, w_down, b_down, final_norm_w):
    bp, tp, d = x_prompt.shape
    bs, ts, _ = x_sample.shape
    n_layers = w_in.shape[0]
    assert bs + 1 <= MOD_ROWS
    p = _layer_params(dict(
        norm1_w=norm1_w, norm2_w=norm2_w, w_in=w_in, ret_decay=ret_decay, rwkv_mu=rwkv_mu, rwkv_w0=rwkv_w0,
        rwkv_w_up=rwkv_w_up, rwkv_a0=rwkv_a0, rwkv_a_up=rwkv_a_up, rwkv_g_up=rwkv_g_up, rwkv_k_k=rwkv_k_k,
        rwkv_k_a=rwkv_k_a, rwkv_r_k=rwkv_r_k, rwkv_ln_w=rwkv_ln_w, q_norm_w=q_norm_w, k_norm_w=k_norm_w,
        w_branch=w_branch, w_out=w_out, router_w=router_w, router_b=router_b, w_gate_up=w_gate_up,
        b_gate_up=b_gate_up, w_down=w_down, b_down=b_down), n_layers)
    cond_rows = jnp.zeros((MOD_ROWS, d), F32).at[0].set(c_ctx).at[1:1 + bs].set(c)
    mod = modulation(cond_rows, mod_w, mod_b)
    rope = _rope_tables(ts)
    final_w = final_norm_w.reshape(1, d)
    np_, ns = bp * tp, bs * ts

    xp = x_prompt.reshape(np_, d)
    xs = x_sample.reshape(ns, d)
    ks_, vs_, srs, sws = [], [], [], []
    for layer in range(n_layers):
        (x1p, h2p, tip, tpp), (k_l, v_l, sr_l, sw_l) = _mixer(xp, p, mod, layer, bp, tp, 0, None)
        (x1s, h2s, tis, tps), _ = _mixer(xs, p, mod, layer, bs, ts, 1,
                                         (rope, cache_k, cache_v, state_ret, state_rwkv))
        ks_.append(k_l.reshape(bp, tp, ATT_KV_HEADS, HEAD_DIM))
        vs_.append(v_l.reshape(bp, tp, ATT_KV_HEADS, HEAD_DIM))
        srs.append(sr_l)
        sws.append(sw_l)
        pos, tile_expert, n_used, last_tile = route(jnp.concatenate([tip, tis], axis=0), MOE_TILE)
        pos_flat = pos.reshape(-1)
        xs_sorted = dispatch(h2p, h2s, pos_flat, last_tile, n_used, MOE_TILE)
        ys = moe_experts(xs_sorted, tile_expert, n_used, p, layer, MOE_TILE)
        final = layer == n_layers - 1
        xp = combine(x1p, tpp, pos_flat, 0, ys, mod, final_w, layer, seq_len=tp, mod_base=0, final=final)
        xs = combine(x1s, tps, pos_flat, np_, ys, mod, final_w, layer, seq_len=ts, mod_base=1, final=final)
    return (xp.reshape(bp, tp, d), xs.reshape(bs, ts, d),
            jnp.stack(ks_, axis=1), jnp.stack(vs_, axis=1), jnp.stack(srs, axis=1), jnp.stack(sws, axis=1))
```
